```python
import jax, jax.numpy as jnp
from jax import lax
import numpy as np

D_MODEL = 4096
BATCH = 4
SEQ = 2048
DEPTH = 2

N_HEADS_MLA = 16
Q_LORA_RANK = 1024
KV_LORA_RANK = 512
QK_NOPE_DIM = 128
QK_ROPE_DIM = 64
V_HEAD_DIM = 128
ROPE_THETA = 10000.0
Q_BLOCK = 128
MLA_OUT_WIDTH = N_HEADS_MLA * V_HEAD_DIM
N_GMLP_GROUPS = 16
GMLP_GROUP_DIM = 128
GMLP_WIDTH = N_GMLP_GROUPS * GMLP_GROUP_DIM
CHUNK = 128
IN_WIDTH = Q_LORA_RANK + KV_LORA_RANK + QK_ROPE_DIM + 2 * GMLP_WIDTH + 2 * D_MODEL
D_FF_DENSE = 11008
N_EXPERTS = 8
TOP_K = 2
D_FF_EXPERT = 4096
EXPERT_BLOCK = 512
N_DENSE_LAYERS = (DEPTH + 1) // 2
N_MOE_LAYERS = DEPTH // 2
NORM_EPS = 1e-6

kernel_name = "hybrid_mla_gmlp_moe_encoder"


def rms_norm(x, g):
    xf = x.astype(jnp.float32)
    y = xf * lax.rsqrt(jnp.mean(xf * xf, axis=-1, keepdims=True) + NORM_EPS)
    return (y * g.astype(jnp.float32)).astype(x.dtype)


def layer_norm(x, g, b):
    xf = x.astype(jnp.float32)
    mu = jnp.mean(xf, axis=-1, keepdims=True)
    xc = xf - mu
    y = xc * lax.rsqrt(jnp.mean(xc * xc, axis=-1, keepdims=True) + NORM_EPS)
    return (y * g.astype(jnp.float32) + b.astype(jnp.float32)).astype(x.dtype)


def rope_tables(positions):
    inv_freq = ROPE_THETA ** (-jnp.arange(0, QK_ROPE_DIM, 2, dtype=jnp.float32) / QK_ROPE_DIM)
    ang = positions.astype(jnp.float32)[..., None] * inv_freq
    return jnp.cos(ang), jnp.sin(ang)


def apply_rope(t, cos, sin):
    t1, t2 = jnp.split(t.astype(jnp.float32), 2, axis=-1)
    out = jnp.concatenate([t1 * cos - t2 * sin, t1 * sin + t2 * cos], axis=-1)
    return out.astype(t.dtype)


def mla_attention(q_nope, q_rope, k_nope, k_rope, v):
    B, S, H, _ = q_nope.shape
    nqb = S // Q_BLOCK
    scale = (QK_NOPE_DIM + QK_ROPE_DIM) ** -0.5
    qn_blocks = q_nope.reshape(B, nqb, Q_BLOCK, H, QK_NOPE_DIM).swapaxes(0, 1)
    qr_blocks = q_rope.reshape(B, nqb, Q_BLOCK, H, QK_ROPE_DIM).swapaxes(0, 1)

    def attend(args):
        qn, qr = args
        s = (jnp.einsum('bqhd,bkhd->bhqk', qn, k_nope)
             + jnp.einsum('bqhr,bkr->bhqk', qr, k_rope))
        p = jax.nn.softmax(s.astype(jnp.float32) * scale, axis=-1).astype(v.dtype)
        return jnp.einsum('bhqk,bkhd->bqhd', p, v)

    o = lax.map(attend, (qn_blocks, qr_blocks))
    return o.swapaxes(0, 1).reshape(B, S, H * V_HEAD_DIM)


def spatial_gate(v, w_s, b_s):
    B, S, _ = v.shape
    vc = v.reshape(B, S // CHUNK, CHUNK, N_GMLP_GROUPS, GMLP_GROUP_DIM)
    s = jnp.einsum('gpq,bcqgd->bcpgd', w_s, vc) + b_s.T[None, None, :, :, None]
    return s.reshape(B, S, GMLP_WIDTH)


def hybrid_mixer(h, cos, sin, w_in, q_norm, kv_norm, w_uq, w_ukv, v_norm_g, v_norm_b,
                 w_sp, b_sp, w_branch_a, w_branch_b, w_out):
    B, S, _ = h.shape
    H = N_HEADS_MLA
    z = h @ w_in
    o1 = Q_LORA_RANK
    o2 = o1 + KV_LORA_RANK
    o3 = o2 + QK_ROPE_DIM
    o4 = o3 + 2 * GMLP_WIDTH
    c_q, c_kv, k_r, uv, gate_pre = jnp.split(z, [o1, o2, o3, o4], axis=-1)

    q = (rms_norm(c_q, q_norm) @ w_uq).reshape(B, S, H, QK_NOPE_DIM + QK_ROPE_DIM)
    q_nope = q[..., :QK_NOPE_DIM]
    q_rope = apply_rope(q[..., QK_NOPE_DIM:], cos[:, :, None, :], sin[:, :, None, :])
    kv = (rms_norm(c_kv, kv_norm) @ w_ukv).reshape(B, S, H, QK_NOPE_DIM + V_HEAD_DIM)
    k_nope = kv[..., :QK_NOPE_DIM]
    v = kv[..., QK_NOPE_DIM:]
    k_rope = apply_rope(k_r, cos, sin)
    y_a = mla_attention(q_nope, q_rope, k_nope, k_rope, v)

    uv = jax.nn.gelu(uv, approximate=False)
    u, vg = jnp.split(uv, 2, axis=-1)
    y_b = u * spatial_gate(layer_norm(vg, v_norm_g, v_norm_b), w_sp, b_sp)

    g_a, g_b = jnp.split(jax.nn.sigmoid(gate_pre), 2, axis=-1)
    m = g_a * (y_a @ w_branch_a) + g_b * (y_b @ w_branch_b)
    return m @ w_out


def swiglu(x, w1, w3, w2):
    return (jax.nn.silu(x @ w1) * (x @ w3)) @ w2


def moe_swiglu(h, w_router, w1, w3, w2):
    B, S, D = h.shape
    T = B * S
    A = T * TOP_K
    ht = h.reshape(T, D)
    logits = (ht @ w_router).astype(jnp.float32)
    top_logits, top_idx = lax.top_k(logits, TOP_K)
    gates = jax.nn.softmax(top_logits, axis=-1)

    flat_e = top_idx.reshape(A).astype(jnp.int32)
    flat_tok = jnp.arange(A, dtype=jnp.int32) // TOP_K
    flat_g = gates.reshape(A)
    order = jnp.argsort(flat_e)
    se = flat_e[order]
    counts = jnp.zeros((N_EXPERTS,), jnp.int32).at[flat_e].add(1)
    padded = (counts + EXPERT_BLOCK - 1) // EXPERT_BLOCK * EXPERT_BLOCK
    pad_end = jnp.cumsum(padded)
    pad_start = pad_end - padded
    grp_start = jnp.cumsum(counts) - counts
    dest = pad_start[se] + (jnp.arange(A, dtype=jnp.int32) - grp_start[se])

    n_blocks = -(-A // EXPERT_BLOCK) + N_EXPERTS
    n_slots = n_blocks * EXPERT_BLOCK
    slot_tok = jnp.zeros((n_slots,), jnp.int32).at[dest].set(flat_tok[order])
    slot_gate = jnp.zeros((n_slots,), jnp.float32).at[dest].set(flat_g[order])
    block_start = jnp.arange(n_blocks, dtype=jnp.int32) * EXPERT_BLOCK
    block_exp = jnp.minimum(jnp.searchsorted(pad_end, block_start, side='right'),
                            N_EXPERTS - 1).astype(jnp.int32)

    xb = ht[slot_tok].reshape(n_blocks, EXPERT_BLOCK, D)

    def expert_block(args):
        xblk, e = args
        return swiglu(xblk, w1[e], w3[e], w2[e])

    yb = lax.map(expert_block, (xb, block_exp)).reshape(n_slots, D)
    out = jnp.zeros((T, D), jnp.float32).at[slot_tok].add(yb.astype(jnp.float32) * slot_gate[:, None])
    return out.astype(h.dtype).reshape(B, S, D)


def setup_inputs(seed: int = 0) -> dict:
    key = jax.random.key(seed)
    ks = iter(jax.random.split(key, 32))
    f32 = jnp.float32

    def nrm(shape, scale):
        return jax.random.normal(next(ks), shape, f32) * scale

    def gain(shape):
        return 1.0 + 0.02 * jax.random.normal(next(ks), shape, f32)

    x = jax.random.normal(next(ks), (BATCH, SEQ, D_MODEL), f32)
    offs = jax.random.randint(next(ks), (BATCH, 1), 0, 4096, dtype=jnp.int32)
    positions = (jnp.arange(SEQ, dtype=jnp.int32)[None, :] + offs).astype(jnp.int32)
    return {
        "x": x,
        "positions": positions,
        "mix_norm": gain((DEPTH, D_MODEL)),
        "w_in": nrm((DEPTH, D_MODEL, IN_WIDTH), D_MODEL ** -0.5),
        "q_norm": gain((DEPTH, Q_LORA_RANK)),
        "kv_norm": gain((DEPTH, KV_LORA_RANK)),
        "w_uq": nrm((DEPTH, Q_LORA_RANK, N_HEADS_MLA * (QK_NOPE_DIM + QK_ROPE_DIM)), Q_LORA_RANK ** -0.5),
        "w_ukv": nrm((DEPTH, KV_LORA_RANK, N_HEADS_MLA * (QK_NOPE_DIM + V_HEAD_DIM)), KV_LORA_RANK ** -0.5),
        "v_norm_g": gain((DEPTH, GMLP_WIDTH)),
        "v_norm_b": nrm((DEPTH, GMLP_WIDTH), 0.02),
        "w_sp": nrm((DEPTH, N_GMLP_GROUPS, CHUNK, CHUNK), CHUNK ** -0.5),
        "b_sp": gain((DEPTH, N_GMLP_GROUPS, CHUNK)),
        "w_branch_a": nrm((DEPTH, MLA_OUT_WIDTH, D_MODEL), MLA_OUT_WIDTH ** -0.5),
        "w_branch_b": nrm((DEPTH, GMLP_WIDTH, D_MODEL), GMLP_WIDTH ** -0.5),
        "w_out": nrm((DEPTH, D_MODEL, D_MODEL), D_MODEL ** -0.5),
        "ffn_norm": gain((DEPTH, D_MODEL)),
        "dense_w1": nrm((N_DENSE_LAYERS, D_MODEL, D_FF_DENSE), D_MODEL ** -0.5),
        "dense_w3": nrm((N_DENSE_LAYERS, D_MODEL, D_FF_DENSE), D_MODEL ** -0.5),
        "dense_w2": nrm((N_DENSE_LAYERS, D_FF_DENSE, D_MODEL), D_FF_DENSE ** -0.5),
        "w_router": nrm((N_MOE_LAYERS, D_MODEL, N_EXPERTS), D_MODEL ** -0.5),
        "moe_w1": nrm((N_MOE_LAYERS, N_EXPERTS, D_MODEL, D_FF_EXPERT), D_MODEL ** -0.5),
        "moe_w3": nrm((N_MOE_LAYERS, N_EXPERTS, D_MODEL, D_FF_EXPERT), D_MODEL ** -0.5),
        "moe_w2": nrm((N_MOE_LAYERS, N_EXPERTS, D_FF_EXPERT, D_MODEL), D_FF_EXPERT ** -0.5),
        "final_norm": gain((D_MODEL,)),
    }


def reference(x, positions, mix_norm, w_in, q_norm, kv_norm, w_uq, w_ukv, v_norm_g, v_norm_b,
              w_sp, b_sp, w_branch_a, w_branch_b, w_out, ffn_norm, dense_w1, dense_w3, dense_w2,
              w_router, moe_w1, moe_w3, moe_w2, final_norm):
    cos, sin = rope_tables(positions)
    for layer in range(DEPTH):
        h = rms_norm(x, mix_norm[layer])
        x = x + hybrid_mixer(h, cos, sin, w_in[layer], q_norm[layer], kv_norm[layer],
                             w_uq[layer], w_ukv[layer], v_norm_g[layer], v_norm_b[layer],
                             w_sp[layer], b_sp[layer], w_branch_a[layer], w_branch_b[layer],
                             w_out[layer])
        h = rms_norm(x, ffn_norm[layer])
        i = layer // 2
        if layer % 2 == 0:
            x = x + swiglu(h, dense_w1[i], dense_w3[i], dense_w2[i])
        else:
            x = x + moe_swiglu(h, w_router[i], moe_w1[i], moe_w3[i], moe_w2[i])
    return rms_norm(x, final_norm)
```

```python
import functools

import jax
import jax.numpy as jnp
from jax import lax
from jax.experimental import pallas as pl
from jax.experimental.pallas import tpu as pltpu

F32 = jnp.float32
BF16 = jnp.bfloat16

V7X_LANES = 128
V7X_VMEM_BYTES = 64 * 1024 * 1024
V7X_VMEM_CAP = V7X_VMEM_BYTES - 8 * 1024 * 1024

QK_NOPE = 128
QK_ROPE = 64
V_HEAD = 128
HEAD_PAD = 256
GROUP_DIM = 128
CHUNK = 128
TOP_K = 2
EXPERT_BLOCK = 512
NORM_EPS = 1e-6
ROPE_THETA = 10000.0


def _tile(n, pref, unit=V7X_LANES):
    t = (min(pref, n) // unit) * unit
    while t >= unit:
        if n % t == 0:
            return t
        t -= unit
    return n


def _vmem_limit(block_bytes, temp_bytes):
    return int(min(V7X_VMEM_CAP, 2 * block_bytes + temp_bytes + (4 << 20)))


def _nbytes(shape, dtype):
    n = 1
    for s in shape:
        n *= s
    return n * jnp.dtype(dtype).itemsize


def _mm_body(*refs, npf, nx, ws_x, epi):
    refs = refs[npf:]
    x_refs = refs[:nx]
    w_refs = refs[nx:nx + len(ws_x)]
    e_refs = refs[nx + len(ws_x):-1]
    o_ref = refs[-1]
    accs = [jnp.dot(x_refs[xi][...], w_ref[...], preferred_element_type=F32)
            for xi, w_ref in zip(ws_x, w_refs)]
    o_ref[...] = epi(accs, [e[...] for e in e_refs]).astype(o_ref.dtype)


def _mm(xs, ws, epi, extras, n, out_dtype, *, tm, tn, group=None, name):
    m = xs[0].shape[0]
    grid = (m // tm, n // tn)
    npf = 0 if group is None else 1
    in_specs = []
    blk = 0
    for x in xs:
        in_specs.append(pl.BlockSpec((tm, x.shape[1]), lambda i, j, *pf: (i, 0)))
        blk += _nbytes((tm, x.shape[1]), x.dtype)
    for _, w, off in ws:
        if group is None:
            in_specs.append(pl.BlockSpec((w.shape[0], tn), lambda i, j, *pf, off=off: (0, j + off)))
            blk += _nbytes((w.shape[0], tn), w.dtype)
        else:
            in_specs.append(pl.BlockSpec((None, w.shape[1], tn), lambda i, j, g: (g[i], 0, j)))
            blk += _nbytes((w.shape[1], tn), w.dtype)
    for arr, kind, off in extras:
        if kind == "mn":
            in_specs.append(pl.BlockSpec((tm, tn), lambda i, j, *pf, off=off: (i, j + off)))
            blk += _nbytes((tm, tn), arr.dtype)
        else:
            in_specs.append(pl.BlockSpec((tm, arr.shape[1]), lambda i, j, *pf: (i, 0)))
            blk += _nbytes((tm, arr.shape[1]), arr.dtype)
    out_spec = pl.BlockSpec((tm, tn), lambda i, j, *pf: (i, j))
    blk += _nbytes((tm, tn), out_dtype)
    body = functools.partial(_mm_body, npf=npf, nx=len(xs), ws_x=tuple(xi for xi, _, _ in ws), epi=epi)
    call = pl.pallas_call(
        body,
        grid_spec=pltpu.PrefetchScalarGridSpec(
            num_scalar_prefetch=npf, grid=grid, in_specs=in_specs, out_specs=out_spec),
        out_shape=jax.ShapeDtypeStruct((m, n), out_dtype),
        compiler_params=pltpu.CompilerParams(
            dimension_semantics=("arbitrary", "arbitrary"),
            vmem_limit_bytes=_vmem_limit(blk, (len(ws) + 2) * tm * tn * 4)),
        name=name,
    )
    args = ([] if group is None else [group]) + list(xs) + [w for _, w, _ in ws] + [a for a, _, _ in extras]
    return call(*args)


def _epi_id(accs, ex):
    return accs[0]


def _epi_gelu(accs, ex):
    a = accs[0]
    return 0.5 * a * (1.0 + lax.erf(a * (2.0 ** -0.5)))


def _epi_sigmoid(accs, ex):
    return jax.nn.sigmoid(accs[0])


def _epi_residual(accs, ex):
    return ex[0] + accs[0]


def _epi_swiglu(accs, ex):
    return jax.nn.silu(accs[0]) * accs[1]


def _epi_gated_merge(accs, ex):
    return ex[0].astype(F32) * accs[0] + ex[1].astype(F32) * accs[1]


def _rope_lanes(t, cos_t, sin_a, sin_b):
    half = QK_ROPE // 2
    return (t * cos_t + pltpu.roll(t, half, 1) * sin_a
            + pltpu.roll(t, V7X_LANES - half, 1) * sin_b)


def _epi_rope_all(accs, ex):
    return _rope_lanes(accs[0], *ex)


def _epi_rope_odd_groups(accs, ex):
    a = accs[0]
    outs = []
    for g in range(a.shape[1] // V7X_LANES):
        t = a[:, g * V7X_LANES:(g + 1) * V7X_LANES]
        outs.append(_rope_lanes(t, *ex) if g % 2 else t)
    return jnp.concatenate(outs, axis=1)


def _rmsnorm_body(x_ref, g_ref, o_ref):
    x = x_ref[...].astype(F32)
    y = x * lax.rsqrt(jnp.mean(x * x, axis=-1, keepdims=True) + NORM_EPS)
    o_ref[...] = (y * g_ref[...]).astype(o_ref.dtype)


def _rmsnorm(x, g, out_dtype, *, width=None, col_block=0, name):
    m = x.shape[0]
    width = x.shape[1] if width is None else width
    tm = _tile(m, 256, 8)
    blk = _nbytes((tm, width), x.dtype) + _nbytes((tm, width), out_dtype)
    return pl.pallas_call(
        _rmsnorm_body,
        grid=(m // tm,),
        in_specs=[pl.BlockSpec((tm, width), lambda i: (i, col_block)),
                  pl.BlockSpec((1, width), lambda i: (0, 0))],
        out_specs=pl.BlockSpec((tm, width), lambda i: (i, 0)),
        out_shape=jax.ShapeDtypeStruct((m, width), out_dtype),
        compiler_params=pltpu.CompilerParams(
            dimension_semantics=("arbitrary",), vmem_limit_bytes=_vmem_limit(blk, 3 * tm * width * 4)),
        name=name,
    )(x, g.reshape(1, width).astype(F32))


def _norm_router_body(x_ref, g_ref, wr_ref, h_ref, route_ref, *, n_experts):
    x = x_ref[...]
    h = x * lax.rsqrt(jnp.mean(x * x, axis=-1, keepdims=True) + NORM_EPS) * g_ref[...]
    h_ref[...] = h
    logits = jnp.dot(h, wr_ref[...], preferred_element_type=F32, precision=lax.Precision.HIGHEST)
    lane = lax.broadcasted_iota(jnp.int32, logits.shape, 1).astype(F32)
    neg = jnp.float32(-jnp.inf)
    far = jnp.float32(V7X_LANES)
    l1 = jnp.where(lane < n_experts, logits, neg)
    m1 = jnp.max(l1, axis=-1, keepdims=True)
    i1 = jnp.min(jnp.where(l1 == m1, lane, far), axis=-1, keepdims=True)
    l2 = jnp.where(lane == i1, neg, l1)
    m2 = jnp.max(l2, axis=-1, keepdims=True)
    i2 = jnp.min(jnp.where(l2 == m2, lane, far), axis=-1, keepdims=True)
    e = jnp.exp(m2 - m1)
    g1 = 1.0 / (1.0 + e)
    g2 = e / (1.0 + e)
    route = jnp.where(lane == 0, i1, jnp.where(lane == 1, i2, jnp.where(lane == 2, g1, jnp.where(lane == 3, g2, 0.0))))
    route_ref[...] = route


def _norm_router(x, g, w_router, *, name):
    m, d = x.shape
    n_experts = w_router.shape[1]
    wr = jnp.pad(w_router.astype(F32), ((0, 0), (0, V7X_LANES - n_experts)))
    tm = _tile(m, 256, 8)
    blk = 2 * _nbytes((tm, d), F32) + _nbytes((d, V7X_LANES), F32) + _nbytes((tm, V7X_LANES), F32)
    return pl.pallas_call(
        functools.partial(_norm_router_body, n_experts=n_experts),
        grid=(m // tm,),
        in_specs=[pl.BlockSpec((tm, d), lambda i: (i, 0)),
                  pl.BlockSpec((1, d), lambda i: (0, 0)),
                  pl.BlockSpec((d, V7X_LANES), lambda i: (0, 0))],
        out_specs=[pl.BlockSpec((tm, d), lambda i: (i, 0)),
                   pl.BlockSpec((tm, V7X_LANES), lambda i: (i, 0))],
        out_shape=[jax.ShapeDtypeStruct((m, d), F32), jax.ShapeDtypeStruct((m, V7X_LANES), F32)],
        compiler_params=pltpu.CompilerParams(
            dimension_semantics=("arbitrary",), vmem_limit_bytes=_vmem_limit(blk, 4 * tm * d * 4)),
        name=name,
    )(x, g.reshape(1, d).astype(F32), wr)


def _attn_body(q_ref, kn_ref, kr_ref, v_ref, o_ref, kcat_ref, *, scale):
    @pl.when(pl.program_id(2) == 0)
    def _():
        kcat_ref[:, :QK_NOPE] = kn_ref[...]
        kcat_ref[:, QK_NOPE:] = kr_ref[...]

    s = lax.dot_general(q_ref[...], kcat_ref[...], (((1,), (1,)), ((), ())),
                        preferred_element_type=F32) * scale
    p = jnp.exp(s - jnp.max(s, axis=-1, keepdims=True))
    l = jnp.sum(p, axis=-1, keepdims=True)
    o = jnp.dot(p.astype(BF16), v_ref[...], preferred_element_type=F32)
    o_ref[...] = (o / l).astype(o_ref.dtype)


def _attention(q, kv, k_rope, *, batch, seq, heads, name):
    t = batch * seq
    tq = _tile(seq, 256, 8)
    nq = seq // tq
    scale = float(QK_NOPE + QK_ROPE) ** -0.5
    blk = (_nbytes((tq, HEAD_PAD), BF16) + 3 * _nbytes((seq, V7X_LANES), BF16) + _nbytes((tq, V_HEAD), BF16))
    return pl.pallas_call(
        functools.partial(_attn_body, scale=scale),
        grid=(batch, heads, nq),
        in_specs=[pl.BlockSpec((tq, HEAD_PAD), lambda b, h, i: (b * nq + i, h)),
                  pl.BlockSpec((seq, QK_NOPE), lambda b, h, i: (b, 2 * h)),
                  pl.BlockSpec((seq, V7X_LANES), lambda b, h, i: (b, 0)),
                  pl.BlockSpec((seq, V_HEAD), lambda b, h, i: (b, 2 * h + 1))],
        out_specs=pl.BlockSpec((tq, V_HEAD), lambda b, h, i: (b * nq + i, h)),
        out_shape=jax.ShapeDtypeStruct((t, heads * V_HEAD), BF16),
        scratch_shapes=[pltpu.VMEM((seq, HEAD_PAD), BF16)],
        compiler_params=pltpu.CompilerParams(
            dimension_semantics=("arbitrary", "arbitrary", "arbitrary"),
            vmem_limit_bytes=_vmem_limit(blk, _nbytes((seq, HEAD_PAD), BF16) + 4 * tq * seq * 4)),
        name=name,
    )(q, kv, k_rope, kv)


def _gmlp_body(u_ref, v_ref, g_ref, b_ref, w_ref, bs_ref, o_ref, *, n_chunks, n_groups):
    v = v_ref[...].astype(F32)
    mu = jnp.mean(v, axis=-1, keepdims=True)
    vc = v - mu
    vn = vc * lax.rsqrt(jnp.mean(vc * vc, axis=-1, keepdims=True) + NORM_EPS)
    vn = (vn * g_ref[...] + b_ref[...]).astype(BF16)
    for c in range(n_chunks):
        rows = slice(c * CHUNK, (c + 1) * CHUNK)
        for g in range(n_groups):
            cols = slice(g * GROUP_DIM, (g + 1) * GROUP_DIM)
            s = jnp.dot(w_ref[g], vn[rows, cols], preferred_element_type=F32) + bs_ref[g]
            o_ref[rows, cols] = (u_ref[rows, cols].astype(F32) * s).astype(o_ref.dtype)


def _gmlp(uv, v_norm_g, v_norm_b, w_sp, b_sp, *, name):
    t = uv.shape[0]
    gw = uv.shape[1] // 2
    n_groups = w_sp.shape[0]
    rows = _tile(t, 2 * CHUNK, CHUNK)
    b_full = jnp.broadcast_to(b_sp.astype(F32)[:, :, None], (n_groups, CHUNK, GROUP_DIM))
    blk = 3 * _nbytes((rows, gw), BF16) + _nbytes(w_sp.shape, BF16) + _nbytes(b_full.shape, F32)
    return pl.pallas_call(
        functools.partial(_gmlp_body, n_chunks=rows // CHUNK, n_groups=n_groups),
        grid=(t // rows,),
        in_specs=[pl.BlockSpec((rows, gw), lambda i: (i, 0)),
                  pl.BlockSpec((rows, gw), lambda i: (i, 1)),
                  pl.BlockSpec((1, gw), lambda i: (0, 0)),
                  pl.BlockSpec((1, gw), lambda i: (0, 0)),
                  pl.BlockSpec((n_groups, CHUNK, CHUNK), lambda i: (0, 0, 0)),
                  pl.BlockSpec((n_groups, CHUNK, GROUP_DIM), lambda i: (0, 0, 0))],
        out_specs=pl.BlockSpec((rows, gw), lambda i: (i, 0)),
        out_shape=jax.ShapeDtypeStruct((t, gw), BF16),
        compiler_params=pltpu.CompilerParams(
            dimension_semantics=("arbitrary",), vmem_limit_bytes=_vmem_limit(blk, 4 * rows * gw * 4)),
        name=name,
    )(uv, uv, v_norm_g.reshape(1, gw).astype(F32), v_norm_b.reshape(1, gw).astype(F32),
      w_sp.astype(BF16), b_full)


def _row_copy(src_hbm, dst_vmem, src_row, dst_row, sem):
    return pltpu.make_async_copy(src_hbm.at[pl.ds(src_row, 1)], dst_vmem.at[pl.ds(dst_row, 1)], sem)


def _gather_body(tok_ref, h_hbm, o_ref, buf, sem, *, rows):
    base = pl.program_id(0) * rows

    def issue(k, c):
        _row_copy(h_hbm, buf, tok_ref[base + k], k, sem).start()
        return c

    def drain(k, c):
        _row_copy(h_hbm, buf, 0, k, sem).wait()
        return c

    lax.fori_loop(0, rows, issue, 0)
    lax.fori_loop(0, rows, drain, 0)
    o_ref[...] = buf[...].astype(o_ref.dtype)


def _gather_rows(h, slot_tok, *, name):
    n_slots = slot_tok.shape[0]
    d = h.shape[1]
    rows = EXPERT_BLOCK
    blk = _nbytes((rows, d), BF16)
    return pl.pallas_call(
        functools.partial(_gather_body, rows=rows),
        grid_spec=pltpu.PrefetchScalarGridSpec(
            num_scalar_prefetch=1, grid=(n_slots // rows,),
            in_specs=[pl.BlockSpec(memory_space=pl.ANY)],
            out_specs=pl.BlockSpec((rows, d), lambda r, tok: (r, 0)),
            scratch_shapes=[pltpu.VMEM((rows, d), F32), pltpu.SemaphoreType.DMA(())]),
        out_shape=jax.ShapeDtypeStruct((n_slots, d), BF16),
        compiler_params=pltpu.CompilerParams(
            dimension_semantics=("arbitrary",),
            vmem_limit_bytes=_vmem_limit(blk, 2 * _nbytes((rows, d), F32))),
        name=name,
    )(slot_tok, h)


def _combine_body(slot_ref, x_ref, route_ref, g_ref, y_hbm, o_ref, buf, sem, *, rows, norm):
    base = pl.program_id(0) * rows

    def issue(k, c):
        for j in range(TOP_K):
            pltpu.make_async_copy(y_hbm.at[pl.ds(slot_ref[(base + k) * TOP_K + j], 1)],
                                  buf.at[j, pl.ds(k, 1)], sem).start()
        return c

    def drain(k, c):
        for j in range(TOP_K):
            pltpu.make_async_copy(y_hbm.at[pl.ds(0, 1)], buf.at[j, pl.ds(k, 1)], sem).wait()
        return c

    lax.fori_loop(0, rows, issue, 0)
    lax.fori_loop(0, rows, drain, 0)
    route = route_ref[...]
    moe = buf[0] * route[:, 2:3] + buf[1] * route[:, 3:4]
    y = x_ref[...] + moe
    if norm:
        y = y * lax.rsqrt(jnp.mean(y * y, axis=-1, keepdims=True) + NORM_EPS) * g_ref[...]
    o_ref[...] = y


def _combine(x, route, slot_of_assign, yb, norm_gain, *, name):
    t, d = x.shape
    rows = _tile(t, 256, 8)
    norm = norm_gain is not None
    g = (norm_gain if norm else jnp.ones((d,), F32)).reshape(1, d).astype(F32)
    blk = 2 * _nbytes((rows, d), F32) + _nbytes((rows, V7X_LANES), F32)
    return pl.pallas_call(
        functools.partial(_combine_body, rows=rows, norm=norm),
        grid_spec=pltpu.PrefetchScalarGridSpec(
            num_scalar_prefetch=1, grid=(t // rows,),
            in_specs=[pl.BlockSpec((rows, d), lambda i, s: (i, 0)),
                      pl.BlockSpec((rows, V7X_LANES), lambda i, s: (i, 0)),
                      pl.BlockSpec((1, d), lambda i, s: (0, 0)),
                      pl.BlockSpec(memory_space=pl.ANY)],
            out_specs=pl.BlockSpec((rows, d), lambda i, s: (i, 0)),
            scratch_shapes=[pltpu.VMEM((TOP_K, rows, d), F32), pltpu.SemaphoreType.DMA(())]),
        out_shape=jax.ShapeDtypeStruct((t, d), F32),
        compiler_params=pltpu.CompilerParams(
            dimension_semantics=("arbitrary",),
            vmem_limit_bytes=_vmem_limit(blk, (TOP_K + 3) * _nbytes((rows, d), F32))),
        name=name,
    )(slot_of_assign, x, route, g, yb)


def _routing_tables(route, n_experts):
    t = route.shape[0]
    a = t * TOP_K
    flat_e = route[:, :TOP_K].astype(jnp.int32).reshape(a)
    flat_tok = jnp.arange(a, dtype=jnp.int32) // TOP_K
    order = jnp.argsort(flat_e)
    se = flat_e[order]
    counts = jnp.zeros((n_experts,), jnp.int32).at[flat_e].add(1)
    padded = (counts + EXPERT_BLOCK - 1) // EXPERT_BLOCK * EXPERT_BLOCK
    pad_end = jnp.cumsum(padded)
    pad_start = pad_end - padded
    grp_start = jnp.cumsum(counts) - counts
    dest = pad_start[se] + (jnp.arange(a, dtype=jnp.int32) - grp_start[se])
    n_blocks = -(-a // EXPERT_BLOCK) + n_experts
    n_slots = n_blocks * EXPERT_BLOCK
    slot_tok = jnp.zeros((n_slots,), jnp.int32).at[dest].set(flat_tok[order])
    slot_of_assign = jnp.zeros((a,), jnp.int32).at[order].set(dest.astype(jnp.int32))
    block_start = jnp.arange(n_blocks, dtype=jnp.int32) * EXPERT_BLOCK
    block_exp = jnp.minimum(jnp.searchsorted(pad_end, block_start, side="right"),
                            n_experts - 1).astype(jnp.int32)
    return slot_tok, slot_of_assign, block_exp


def _rope_panels(positions):
    half = QK_ROPE // 2
    inv_freq = ROPE_THETA ** (-jnp.arange(0, QK_ROPE, 2, dtype=F32) / QK_ROPE)
    ang = positions.astype(F32).reshape(-1, 1) * inv_freq
    cos, sin = jnp.cos(ang), jnp.sin(ang)
    z = jnp.zeros_like(cos)
    pad = jnp.zeros((cos.shape[0], V7X_LANES - QK_ROPE), F32)
    cos_t = jnp.concatenate([cos, cos, pad], axis=1)
    sin_a = jnp.concatenate([z, sin, pad], axis=1)
    sin_b = jnp.concatenate([-sin, z, pad], axis=1)
    assert cos_t.shape[1] == V7X_LANES and half * 2 == QK_ROPE
    return cos_t, sin_a, sin_b


def kernel(x, positions, mix_norm, w_in, q_norm, kv_norm, w_uq, w_ukv, v_norm_g, v_norm_b, w_sp, b_sp, w_branch_a, w_branch_b, w_out, ffn_norm, dense_w1, dense_w3, dense_w2, w_router, moe_w1, moe_w3, moe_w2, final_norm):
    batch, seq, d = x.shape
    t = batch * seq
    depth = mix_norm.shape[0]
    ql, kvl = q_norm.shape[1], kv_norm.shape[1]
    heads = w_ukv.shape[2] // (QK_NOPE + V_HEAD)
    gw = v_norm_g.shape[1]
    o2 = ql + kvl
    o3 = o2 + QK_ROPE
    o4 = o3 + 2 * gw
    tm = _tile(t, 1024, 8)

    rope = _rope_panels(positions)
    rope_ex = [(p, "m", 0) for p in rope]
    xf = x.reshape(t, d)
    out = None
    for layer in range(depth):
        wl = w_in[layer]
        w_cqkv = wl[:, :o2].astype(BF16)
        w_kr = jnp.pad(wl[:, o2:o3], ((0, 0), (0, V7X_LANES - QK_ROPE))).astype(BF16)
        w_uv = wl[:, o3:o4].astype(BF16)
        w_gate = wl[:, o4:].astype(BF16)
        w_q = jnp.pad(w_uq[layer].reshape(ql, heads, QK_NOPE + QK_ROPE),
                      ((0, 0), (0, 0), (0, HEAD_PAD - QK_NOPE - QK_ROPE))).reshape(ql, heads * HEAD_PAD).astype(BF16)
        w_kv = w_ukv[layer].astype(BF16)

        h = _rmsnorm(xf, mix_norm[layer], BF16, name=f"mix_norm{layer}")
        cqkv = _mm([h], [(0, w_cqkv, 0)], _epi_id, [], o2, F32, tm=tm, tn=_tile(o2, 512), name=f"in_latent{layer}")
        cq = _rmsnorm(cqkv, q_norm[layer], BF16, width=ql, col_block=0, name=f"q_norm{layer}")
        ckv = _rmsnorm(cqkv, kv_norm[layer], BF16, width=kvl, col_block=ql // kvl, name=f"kv_norm{layer}")
        k_rope = _mm([h], [(0, w_kr, 0)], _epi_rope_all, rope_ex, V7X_LANES, BF16, tm=tm, tn=V7X_LANES, name=f"in_krope{layer}")
        uv = _mm([h], [(0, w_uv, 0)], _epi_gelu, [], 2 * gw, BF16, tm=tm, tn=_tile(2 * gw, 512), name=f"in_uv{layer}")
        gates = _mm([h], [(0, w_gate, 0)], _epi_sigmoid, [], 2 * d, BF16, tm=tm, tn=_tile(2 * d, 512), name=f"in_gate{layer}")

        q = _mm([cq], [(0, w_q, 0)], _epi_rope_odd_groups, rope_ex, heads * HEAD_PAD, BF16,
                tm=tm, tn=_tile(heads * HEAD_PAD, 512, 2 * V7X_LANES), name=f"q_up{layer}")
        kv = _mm([ckv], [(0, w_kv, 0)], _epi_id, [], heads * (QK_NOPE + V_HEAD), BF16,
                 tm=tm, tn=_tile(heads * (QK_NOPE + V_HEAD), 512), name=f"kv_up{layer}")
        y_a = _attention(q, kv, k_rope, batch=batch, seq=seq, heads=heads, name=f"attention{layer}")
        y_b = _gmlp(uv, v_norm_g[layer], v_norm_b[layer], w_sp[layer], b_sp[layer], name=f"gmlp{layer}")

        tn_d = _tile(d, 512)
        merged = _mm([y_a, y_b], [(0, w_branch_a[layer].astype(BF16), 0), (1, w_branch_b[layer].astype(BF16), 0)],
                     _epi_gated_merge, [(gates, "mn", 0), (gates, "mn", d // tn_d)], d, BF16,
                     tm=tm, tn=tn_d, name=f"merge{layer}")
        xf = _mm([merged], [(0, w_out[layer].astype(BF16), 0)], _epi_residual, [(xf, "mn", 0)], d, F32,
                 tm=tm, tn=tn_d, name=f"mix_out{layer}")

        i = layer // 2
        last = layer == depth - 1
        if layer % 2 == 0:
            ff = dense_w1.shape[2]
            h = _rmsnorm(xf, ffn_norm[layer], BF16, name=f"ffn_norm{layer}")
            act = _mm([h], [(0, dense_w1[i].astype(BF16), 0), (0, dense_w3[i].astype(BF16), 0)], _epi_swiglu, [],
                      ff, BF16, tm=tm, tn=_tile(ff, 256), name=f"dense_up{layer}")
            xf = _mm([act], [(0, dense_w2[i].astype(BF16), 0)], _epi_residual, [(xf, "mn", 0)], d, F32,
                     tm=_tile(t, 512, 8), tn=_tile(d, 256), name=f"dense_down{layer}")
            if last:
                out = _rmsnorm(xf, final_norm, F32, name="final_norm")
        else:
            n_experts = w_router.shape[2]
            fe = moe_w1.shape[3]
            hf, route = _norm_router(xf, ffn_norm[layer], w_router[i], name=f"ffn_norm_router{layer}")
            slot_tok, slot_of_assign, block_exp = _routing_tables(route, n_experts)
            xb = _gather_rows(hf, slot_tok, name=f"moe_gather{layer}")
            act = _mm([xb], [(0, moe_w1[i].astype(BF16), 0), (0, moe_w3[i].astype(BF16), 0)], _epi_swiglu, [],
                      fe, BF16, tm=EXPERT_BLOCK, tn=_tile(fe, 512), group=block_exp, name=f"moe_up{layer}")
            yb = _mm([act], [(0, moe_w2[i].astype(BF16), 0)], _epi_id, [], d, F32,
                     tm=EXPERT_BLOCK, tn=tn_d, group=block_exp, name=f"moe_down{layer}")
            xf = _combine(xf, route, slot_of_assign, yb, final_norm if last else None, name=f"moe_combine{layer}")
            if last:
                out = xf
    return out.reshape(batch, seq, d)
```

```python
import functools
import math
from typing import Any, NamedTuple

import jax
import jax.numpy as jnp
from jax import lax
from jax.experimental import pallas as pl
from jax.experimental.pallas import tpu as pltpu

F32 = jnp.float32
BF16 = jnp.bfloat16

V7X_LANES = 128
V7X_VMEM_BYTES = 64 * 1024 * 1024
V7X_VMEM_CAP = V7X_VMEM_BYTES - 6 * 1024 * 1024

QK_NOPE = 128
QK_ROPE = 64
V_HEAD = 128
HEAD_PAD = 256
GROUP_DIM = 128
CHUNK = 128
TOP_K = 2
EXPERT_BLOCK = 512
NORM_EPS = 1e-6
ROPE_THETA = 10000.0
Q_PRESCALE = float(QK_NOPE + QK_ROPE) ** -0.5 * math.log2(math.e)


def _tile(n, pref, unit=V7X_LANES):
    t = (min(pref, n) // unit) * unit
    while t >= unit:
        if n % t == 0:
            return t
        t -= unit
    return n


def _vmem_limit(block_bytes, temp_bytes):
    return int(min(V7X_VMEM_CAP, 2 * block_bytes + temp_bytes + (4 << 20)))


def _nbytes(shape, dtype):
    n = 1
    for s in shape:
        n *= s
    return n * jnp.dtype(dtype).itemsize


class W(NamedTuple):
    xi: int
    arr: Any
    lead: Any = None
    off: int = 0
    shift: int = 0


def _mm_body(*refs, npf, nx, ws, ne, epi, cast, grouped):
    pf = refs[:npf]
    refs = refs[npf:]
    x_refs = refs[:nx]
    refs = refs[nx:]
    w_refs = []
    for w in ws:
        take = 2 if w.shift else 1
        w_refs.append(refs[:take])
        refs = refs[take:]
    e_refs = refs[:ne]
    o_ref = refs[ne]
    wbf_refs = refs[ne + 1:]

    if cast:
        i = pl.program_id(1)
        first = i == 0
        if grouped:
            g_ref = pf[0]
            first = jnp.logical_or(first, g_ref[i] != g_ref[jnp.maximum(i - 1, 0)])

        @pl.when(first)
        def _():
            for w, wr, wbf in zip(ws, w_refs, wbf_refs):
                if not w.shift:
                    wbf[...] = wr[0][...].astype(BF16)
                    continue
                k = wr[0].shape[0]
                kc = _tile(k, 512, 8)
                for k0 in range(0, k, kc):
                    rows = slice(k0, k0 + kc)
                    wbf[rows, :] = jnp.concatenate(
                        [wr[0][rows, w.shift:], wr[1][rows, :w.shift]], axis=1).astype(BF16)

        mats = [wbf[...] for wbf in wbf_refs]
    else:
        mats = [wr[0][...] for wr in w_refs]
    accs = [jnp.dot(x_refs[w.xi][...], m, preferred_element_type=F32) for w, m in zip(ws, mats)]
    o_ref[...] = epi(accs, [e[...] for e in e_refs]).astype(o_ref.dtype)


def _mm(xs, ws, epi, extras, n, out_dtype, *, tm, tn, w_resident, group=None, name):
    m = xs[0].shape[0]
    gm, gn = m // tm, n // tn
    npf = 0 if group is None else 1
    if w_resident:
        grid = (gn, gm)
        ij = lambda a, b: (b, a)
    else:
        grid = (gm, gn)
        ij = lambda a, b: (a, b)
        assert all(w.arr.dtype == BF16 and not w.shift for w in ws)

    def imap(f):
        return lambda a, b, *pf: f(*ij(a, b), *pf)

    in_specs = []
    blk = 0
    for x in xs:
        in_specs.append(pl.BlockSpec((tm, x.shape[1]), imap(lambda i, j, *pf: (i, 0))))
        blk += _nbytes((tm, x.shape[1]), x.dtype)
    scratch = []
    for w in ws:
        k = w.arr.shape[-2]
        lead = () if w.lead is None else (w.lead,)
        none = (None,) * len(lead)
        if group is None:
            assert w.arr.ndim == 2 + len(lead)
            in_specs.append(pl.BlockSpec(none + (k, tn), imap(
                lambda i, j, *pf, lead=lead, off=w.off: lead + (0, j + off))))
        else:
            assert w.arr.ndim == 3 + len(lead) and not w.shift
            in_specs.append(pl.BlockSpec(none + (None, k, tn), imap(
                lambda i, j, g, lead=lead: lead + (g[i], 0, j))))
        blk += _nbytes((k, tn), w.arr.dtype)
        if w.shift:
            per = tn // V7X_LANES
            in_specs.append(pl.BlockSpec(none + (k, V7X_LANES), imap(
                lambda i, j, *pf, lead=lead, off=w.off, per=per: lead + (0, (j + off + 1) * per))))
            blk += _nbytes((k, V7X_LANES), w.arr.dtype)
        if w_resident:
            scratch.append(pltpu.VMEM((k, tn), BF16))
    for arr, kind, off in extras:
        if kind == "mn":
            in_specs.append(pl.BlockSpec((tm, tn), imap(lambda i, j, *pf, off=off: (i, j + off))))
            blk += _nbytes((tm, tn), arr.dtype)
        else:
            in_specs.append(pl.BlockSpec((tm, arr.shape[1]), imap(lambda i, j, *pf: (i, 0))))
            blk += _nbytes((tm, arr.shape[1]), arr.dtype)
    out_spec = pl.BlockSpec((tm, tn), imap(lambda i, j, *pf: (i, j)))
    blk += _nbytes((tm, tn), out_dtype)
    temp = (len(ws) + 2) * tm * tn * 4 + sum(_nbytes(s.shape, BF16) for s in scratch)
    body = functools.partial(_mm_body, npf=npf, nx=len(xs), ws=tuple(w._replace(arr=None) for w in ws),
                             ne=len(extras), epi=epi, cast=w_resident, grouped=group is not None)
    call = pl.pallas_call(
        body,
        grid_spec=pltpu.PrefetchScalarGridSpec(
            num_scalar_prefetch=npf, grid=grid, in_specs=in_specs, out_specs=out_spec, scratch_shapes=scratch),
        out_shape=jax.ShapeDtypeStruct((m, n), out_dtype),
        compiler_params=pltpu.CompilerParams(
            dimension_semantics=("arbitrary", "arbitrary"), vmem_limit_bytes=_vmem_limit(blk, temp)),
        name=name,
    )
    args = [] if group is None else [group]
    args += list(xs)
    for w in ws:
        args += [w.arr, w.arr] if w.shift else [w.arr]
    args += [a for a, _, _ in extras]
    return call(*args)


def _epi_id(accs, ex):
    return accs[0]


def _epi_gelu(accs, ex):
    a = accs[0]
    return 0.5 * a * (1.0 + lax.erf(a * (2.0 ** -0.5)))


def _epi_sigmoid(accs, ex):
    return jax.nn.sigmoid(accs[0])


def _epi_residual(accs, ex):
    return ex[0] + accs[0]


def _epi_swiglu(accs, ex):
    return jax.nn.silu(accs[0]) * accs[1]


def _epi_gated_merge(accs, ex):
    return ex[0].astype(F32) * accs[0] + ex[1].astype(F32) * accs[1]


def _rope_lanes(t, cos_t, sin_a, sin_b):
    half = QK_ROPE // 2
    return (t * cos_t + pltpu.roll(t, half, 1) * sin_a
            + pltpu.roll(t, V7X_LANES - half, 1) * sin_b)


def _epi_rope_all(accs, ex):
    return _rope_lanes(accs[0], *ex)


def _epi_q_heads(accs, ex):
    a = accs[0]
    outs = []
    for g in range(a.shape[1] // V7X_LANES):
        t = a[:, g * V7X_LANES:(g + 1) * V7X_LANES]
        outs.append((_rope_lanes(t, *ex) if g % 2 else t) * Q_PRESCALE)
    return jnp.concatenate(outs, axis=1)


def _rmsnorm_body(x_ref, g_ref, o_ref):
    x = x_ref[...].astype(F32)
    y = x * lax.rsqrt(jnp.mean(x * x, axis=-1, keepdims=True) + NORM_EPS)
    o_ref[...] = (y * g_ref[...]).astype(o_ref.dtype)


def _rmsnorm(x, g, out_dtype, *, width=None, col_block=0, name):
    m = x.shape[0]
    width = x.shape[1] if width is None else width
    tm = _tile(m, 256, 8)
    blk = _nbytes((tm, width), x.dtype) + _nbytes((tm, width), out_dtype)
    return pl.pallas_call(
        _rmsnorm_body,
        grid=(m // tm,),
        in_specs=[pl.BlockSpec((tm, width), lambda i: (i, col_block)),
                  pl.BlockSpec((1, width), lambda i: (0, 0))],
        out_specs=pl.BlockSpec((tm, width), lambda i: (i, 0)),
        out_shape=jax.ShapeDtypeStruct((m, width), out_dtype),
        compiler_params=pltpu.CompilerParams(
            dimension_semantics=("arbitrary",), vmem_limit_bytes=_vmem_limit(blk, 3 * tm * width * 4)),
        name=name,
    )(x, g.reshape(1, width).astype(F32))


def _norm_router_body(x_ref, g_ref, wr_ref, h_ref, route_ref, *, n_experts):
    x = x_ref[...]
    h = x * lax.rsqrt(jnp.mean(x * x, axis=-1, keepdims=True) + NORM_EPS) * g_ref[...]
    h_ref[...] = h
    logits = jnp.dot(h, wr_ref[...], preferred_element_type=F32, precision=lax.Precision.HIGHEST)
    lane = lax.broadcasted_iota(jnp.int32, logits.shape, 1).astype(F32)
    neg = jnp.float32(-jnp.inf)
    far = jnp.float32(V7X_LANES)
    l1 = jnp.where(lane < n_experts, logits, neg)
    m1 = jnp.max(l1, axis=-1, keepdims=True)
    i1 = jnp.min(jnp.where(l1 == m1, lane, far), axis=-1, keepdims=True)
    l2 = jnp.where(lane == i1, neg, l1)
    m2 = jnp.max(l2, axis=-1, keepdims=True)
    i2 = jnp.min(jnp.where(l2 == m2, lane, far), axis=-1, keepdims=True)
    e = jnp.exp(m2 - m1)
    g1 = 1.0 / (1.0 + e)
    g2 = e / (1.0 + e)
    route = jnp.where(lane == 0, i1, jnp.where(lane == 1, i2, jnp.where(lane == 2, g1, jnp.where(lane == 3, g2, 0.0))))
    route_ref[...] = route


def _norm_router(x, g, w_router, *, name):
    m, d = x.shape
    n_experts = w_router.shape[1]
    wr = jnp.pad(w_router.astype(F32), ((0, 0), (0, V7X_LANES - n_experts)))
    tm = _tile(m, 256, 8)
    blk = 2 * _nbytes((tm, d), F32) + _nbytes((d, V7X_LANES), F32) + _nbytes((tm, V7X_LANES), F32)
    return pl.pallas_call(
        functools.partial(_norm_router_body, n_experts=n_experts),
        grid=(m // tm,),
        in_specs=[pl.BlockSpec((tm, d), lambda i: (i, 0)),
                  pl.BlockSpec((1, d), lambda i: (0, 0)),
                  pl.BlockSpec((d, V7X_LANES), lambda i: (0, 0))],
        out_specs=[pl.BlockSpec((tm, d), lambda i: (i, 0)),
                   pl.BlockSpec((tm, V7X_LANES), lambda i: (i, 0))],
        out_shape=[jax.ShapeDtypeStruct((m, d), F32), jax.ShapeDtypeStruct((m, V7X_LANES), F32)],
        compiler_params=pltpu.CompilerParams(
            dimension_semantics=("arbitrary",), vmem_limit_bytes=_vmem_limit(blk, 4 * tm * d * 4)),
        name=name,
    )(x, g.reshape(1, d).astype(F32), wr)


def _attn_body(q_ref, kn_ref, kr_ref, v_ref, o_ref, kcat_ref, *, sub):
    @pl.when(pl.program_id(2) == 0)
    def _():
        kcat_ref[:, :QK_NOPE] = kn_ref[...]
        kcat_ref[:, QK_NOPE:] = kr_ref[...]

    for r0 in range(0, q_ref.shape[0], sub):
        rows = slice(r0, r0 + sub)
        s = lax.dot_general(q_ref[rows, :], kcat_ref[...], (((1,), (1,)), ((), ())),
                            preferred_element_type=F32)
        p = jnp.exp2(s - jnp.max(s, axis=-1, keepdims=True))
        l = jnp.sum(p, axis=-1, keepdims=True)
        o = jnp.dot(p.astype(BF16), v_ref[...], preferred_element_type=F32)
        o_ref[rows, :] = (o / l).astype(o_ref.dtype)


def _attention(q, kv, k_rope, *, batch, seq, heads, name):
    t = batch * seq
    sub = _tile(seq, 256, 8)
    tq = _tile(seq, 8 * sub, sub)
    nq = seq // tq
    blk = (_nbytes((tq, HEAD_PAD), BF16) + 3 * _nbytes((seq, V7X_LANES), BF16) + _nbytes((tq, V_HEAD), BF16))
    return pl.pallas_call(
        functools.partial(_attn_body, sub=sub),
        grid=(batch, heads, nq),
        in_specs=[pl.BlockSpec((tq, HEAD_PAD), lambda b, h, i: (b * nq + i, h)),
                  pl.BlockSpec((seq, QK_NOPE), lambda b, h, i: (b, 2 * h)),
                  pl.BlockSpec((seq, V7X_LANES), lambda b, h, i: (b, 0)),
                  pl.BlockSpec((seq, V_HEAD), lambda b, h, i: (b, 2 * h + 1))],
        out_specs=pl.BlockSpec((tq, V_HEAD), lambda b, h, i: (b * nq + i, h)),
        out_shape=jax.ShapeDtypeStruct((t, heads * V_HEAD), BF16),
        scratch_shapes=[pltpu.VMEM((seq, HEAD_PAD), BF16)],
        compiler_params=pltpu.CompilerParams(
            dimension_semantics=("arbitrary", "arbitrary", "arbitrary"),
            vmem_limit_bytes=_vmem_limit(blk, _nbytes((seq, HEAD_PAD), BF16) + 4 * tq * seq * 4)),
        name=name,
    )(q, kv, k_rope, kv)


def _gmlp_body(u_ref, v_ref, g_ref, b_ref, w_ref, bs_ref, o_ref, *, n_chunks, n_groups):
    v = v_ref[...].astype(F32)
    mu = jnp.mean(v, axis=-1, keepdims=True)
    vc = v - mu
    vn = vc * lax.rsqrt(jnp.mean(vc * vc, axis=-1, keepdims=True) + NORM_EPS)
    vn = (vn * g_ref[...] + b_ref[...]).astype(BF16)
    for c in range(n_chunks):
        rows = slice(c * CHUNK, (c + 1) * CHUNK)
        for g in range(n_groups):
            cols = slice(g * GROUP_DIM, (g + 1) * GROUP_DIM)
            s = jnp.dot(w_ref[g], vn[rows, cols], preferred_element_type=F32) + bs_ref[g]
            o_ref[rows, cols] = (u_ref[rows, cols].astype(F32) * s).astype(o_ref.dtype)


def _gmlp(uv, v_norm_g, v_norm_b, w_sp, b_sp, *, name):
    t = uv.shape[0]
    gw = uv.shape[1] // 2
    n_groups = w_sp.shape[0]
    rows = _tile(t, 2 * CHUNK, CHUNK)
    b_full = jnp.broadcast_to(b_sp.astype(F32)[:, :, None], (n_groups, CHUNK, GROUP_DIM))
    blk = 3 * _nbytes((rows, gw), BF16) + _nbytes(w_sp.shape, BF16) + _nbytes(b_full.shape, F32)
    return pl.pallas_call(
        functools.partial(_gmlp_body, n_chunks=rows // CHUNK, n_groups=n_groups),
        grid=(t // rows,),
        in_specs=[pl.BlockSpec((rows, gw), lambda i: (i, 0)),
                  pl.BlockSpec((rows, gw), lambda i: (i, 1)),
                  pl.BlockSpec((1, gw), lambda i: (0, 0)),
                  pl.BlockSpec((1, gw), lambda i: (0, 0)),
                  pl.BlockSpec((n_groups, CHUNK, CHUNK), lambda i: (0, 0, 0)),
                  pl.BlockSpec((n_groups, CHUNK, GROUP_DIM), lambda i: (0, 0, 0))],
        out_specs=pl.BlockSpec((rows, gw), lambda i: (i, 0)),
        out_shape=jax.ShapeDtypeStruct((t, gw), BF16),
        compiler_params=pltpu.CompilerParams(
            dimension_semantics=("arbitrary",), vmem_limit_bytes=_vmem_limit(blk, 4 * rows * gw * 4)),
        name=name,
    )(uv, uv, v_norm_g.reshape(1, gw).astype(F32), v_norm_b.reshape(1, gw).astype(F32),
      w_sp.astype(BF16), b_full)


def _row_copy(src_hbm, dst_vmem, src_row, dst_row, sem):
    return pltpu.make_async_copy(src_hbm.at[pl.ds(src_row, 1)], dst_vmem.at[pl.ds(dst_row, 1)], sem)


def _gather_body(tok_ref, h_hbm, o_ref, buf, sem, *, rows, n_blocks):
    r = pl.program_id(0)

    def issue(block, slot):
        base = block * rows

        def f(k, c):
            _row_copy(h_hbm, buf.at[slot], tok_ref[base + k], k, sem.at[slot]).start()
            return c

        lax.fori_loop(0, rows, f, 0, unroll=8)

    @pl.when(r == 0)
    def _():
        issue(0, 0)

    @pl.when(r + 1 < n_blocks)
    def _():
        issue(r + 1, (r + 1) % 2)

    slot = r % 2

    def drain(k, c):
        _row_copy(h_hbm, buf.at[slot], 0, k, sem.at[slot]).wait()
        return c

    lax.fori_loop(0, rows, drain, 0, unroll=8)
    o_ref[...] = buf[slot].astype(o_ref.dtype)


def _gather_rows(h, slot_tok, *, name):
    n_slots = slot_tok.shape[0]
    d = h.shape[1]
    rows = EXPERT_BLOCK
    n_blocks = n_slots // rows
    blk = _nbytes((rows, d), BF16)
    return pl.pallas_call(
        functools.partial(_gather_body, rows=rows, n_blocks=n_blocks),
        grid_spec=pltpu.PrefetchScalarGridSpec(
            num_scalar_prefetch=1, grid=(n_blocks,),
            in_specs=[pl.BlockSpec(memory_space=pl.ANY)],
            out_specs=pl.BlockSpec((rows, d), lambda r, tok: (r, 0)),
            scratch_shapes=[pltpu.VMEM((2, rows, d), F32), pltpu.SemaphoreType.DMA((2,))]),
        out_shape=jax.ShapeDtypeStruct((n_slots, d), BF16),
        compiler_params=pltpu.CompilerParams(
            dimension_semantics=("arbitrary",),
            vmem_limit_bytes=_vmem_limit(blk, 3 * _nbytes((rows, d), F32))),
        name=name,
    )(slot_tok, h)


def _combine_body(slot_ref, x_ref, route_ref, g_ref, y_hbm, o_ref, buf, sem, *, rows, n_tiles, norm):
    i = pl.program_id(0)

    def issue(tile, slot):
        base = tile * rows

        def f(k, c):
            for j in range(TOP_K):
                _row_copy(y_hbm, buf.at[slot, j], slot_ref[(base + k) * TOP_K + j], k, sem.at[slot]).start()
            return c

        lax.fori_loop(0, rows, f, 0, unroll=4)

    @pl.when(i == 0)
    def _():
        issue(0, 0)

    @pl.when(i + 1 < n_tiles)
    def _():
        issue(i + 1, (i + 1) % 2)

    slot = i % 2

    def drain(k, c):
        for j in range(TOP_K):
            _row_copy(y_hbm, buf.at[slot, j], 0, k, sem.at[slot]).wait()
        return c

    lax.fori_loop(0, rows, drain, 0, unroll=4)
    route = route_ref[...]
    y = x_ref[...] + (buf[slot, 0] * route[:, 2:3] + buf[slot, 1] * route[:, 3:4])
    if norm:
        y = y * lax.rsqrt(jnp.mean(y * y, axis=-1, keepdims=True) + NORM_EPS) * g_ref[...]
    o_ref[...] = y


def _combine(x, route, slot_of_assign, yb, norm_gain, *, name):
    t, d = x.shape
    rows = _tile(t, 256, 8)
    n_tiles = t // rows
    norm = norm_gain is not None
    g = (norm_gain if norm else jnp.ones((d,), F32)).reshape(1, d).astype(F32)
    blk = 2 * _nbytes((rows, d), F32) + _nbytes((rows, V7X_LANES), F32)
    return pl.pallas_call(
        functools.partial(_combine_body, rows=rows, n_tiles=n_tiles, norm=norm),
        grid_spec=pltpu.PrefetchScalarGridSpec(
            num_scalar_prefetch=1, grid=(n_tiles,),
            in_specs=[pl.BlockSpec((rows, d), lambda i, s: (i, 0)),
                      pl.BlockSpec((rows, V7X_LANES), lambda i, s: (i, 0)),
                      pl.BlockSpec((1, d), lambda i, s: (0, 0)),
                      pl.BlockSpec(memory_space=pl.ANY)],
            out_specs=pl.BlockSpec((rows, d), lambda i, s: (i, 0)),
            scratch_shapes=[pltpu.VMEM((2, TOP_K, rows, d), F32), pltpu.SemaphoreType.DMA((2,))]),
        out_shape=jax.ShapeDtypeStruct((t, d), F32),
        compiler_params=pltpu.CompilerParams(
            dimension_semantics=("arbitrary",),
            vmem_limit_bytes=_vmem_limit(blk, (2 * TOP_K + 3) * _nbytes((rows, d), F32))),
        name=name,
    )(slot_of_assign, x, route, g, yb)


def _routing_tables(route, n_experts):
    t = route.shape[0]
    a = t * TOP_K
    flat_e = route[:, :TOP_K].astype(jnp.int32).reshape(a)
    flat_tok = jnp.arange(a, dtype=jnp.int32) // TOP_K
    order = jnp.argsort(flat_e)
    se = flat_e[order]
    counts = jnp.zeros((n_experts,), jnp.int32).at[flat_e].add(1)
    padded = (counts + EXPERT_BLOCK - 1) // EXPERT_BLOCK * EXPERT_BLOCK
    pad_end = jnp.cumsum(padded)
    pad_start = pad_end - padded
    grp_start = jnp.cumsum(counts) - counts
    dest = pad_start[se] + (jnp.arange(a, dtype=jnp.int32) - grp_start[se])
    n_blocks = -(-a // EXPERT_BLOCK) + n_experts
    n_slots = n_blocks * EXPERT_BLOCK
    slot_tok = jnp.zeros((n_slots,), jnp.int32).at[dest].set(flat_tok[order])
    slot_of_assign = jnp.zeros((a,), jnp.int32).at[order].set(dest.astype(jnp.int32))
    block_start = jnp.arange(n_blocks, dtype=jnp.int32) * EXPERT_BLOCK
    block_exp = jnp.minimum(jnp.searchsorted(pad_end, block_start, side="right"),
                            n_experts - 1).astype(jnp.int32)
    return slot_tok, slot_of_assign, block_exp


def _rope_panels(positions):
    inv_freq = ROPE_THETA ** (-jnp.arange(0, QK_ROPE, 2, dtype=F32) / QK_ROPE)
    ang = positions.astype(F32).reshape(-1, 1) * inv_freq
    cos, sin = jnp.cos(ang), jnp.sin(ang)
    z = jnp.zeros_like(cos)
    pad = jnp.zeros((cos.shape[0], V7X_LANES - QK_ROPE), F32)
    cos_t = jnp.concatenate([cos, cos, pad], axis=1)
    sin_a = jnp.concatenate([z, sin, pad], axis=1)
    sin_b = jnp.concatenate([-sin, z, pad], axis=1)
    return cos_t, sin_a, sin_b


def kernel(x, positions, mix_norm, w_in, q_norm, kv_norm, w_uq, w_ukv, v_norm_g, v_norm_b, w_sp, b_sp, w_branch_a, w_branch_b, w_out, ffn_norm, dense_w1, dense_w3, dense_w2, w_router, moe_w1, moe_w3, moe_w2, final_norm):
    batch, seq, d = x.shape
    t = batch * seq
    depth = mix_norm.shape[0]
    ql, kvl = q_norm.shape[1], kv_norm.shape[1]
    heads = w_ukv.shape[2] // (QK_NOPE + V_HEAD)
    gw = v_norm_g.shape[1]
    o2 = ql + kvl
    o3 = o2 + QK_ROPE
    o4 = o3 + 2 * gw
    assert ql % kvl == 0 and o2 % V7X_LANES == 0
    tm = _tile(t, 1024, 8)
    tn_in = _tile(math.gcd(o2, o4 - QK_ROPE, 2 * gw, 2 * d), 512)
    tn_d = _tile(d, 512)

    rope = _rope_panels(positions)
    rope_ex = [(p, "m", 0) for p in rope]
    xf = x.reshape(t, d)
    out = None
    for layer in range(depth):
        w_q = jnp.pad(w_uq[layer].reshape(ql, heads, QK_NOPE + QK_ROPE),
                      ((0, 0), (0, 0), (0, HEAD_PAD - QK_NOPE - QK_ROPE))).reshape(ql, heads * HEAD_PAD).astype(BF16)
        w_kv = w_ukv[layer].astype(BF16)

        h = _rmsnorm(xf, mix_norm[layer], BF16, name=f"mix_norm{layer}")
        cqkv = _mm([h], [W(0, w_in, layer)], _epi_id, [], o2, F32,
                   tm=tm, tn=_tile(o2, 512), w_resident=True, name=f"in_latent{layer}")
        cq = _rmsnorm(cqkv, q_norm[layer], BF16, width=ql, col_block=0, name=f"q_norm{layer}")
        ckv = _rmsnorm(cqkv, kv_norm[layer], BF16, width=kvl, col_block=ql // kvl, name=f"kv_norm{layer}")
        k_rope = _mm([h], [W(0, w_in, layer, off=o2 // V7X_LANES)], _epi_rope_all, rope_ex, V7X_LANES, BF16,
                     tm=tm, tn=V7X_LANES, w_resident=True, name=f"in_krope{layer}")
        uv = _mm([h], [W(0, w_in, layer, off=o2 // tn_in, shift=QK_ROPE)], _epi_gelu, [], 2 * gw, BF16,
                 tm=tm, tn=tn_in, w_resident=True, name=f"in_uv{layer}")
        gates = _mm([h], [W(0, w_in, layer, off=(o4 - QK_ROPE) // tn_in, shift=QK_ROPE)], _epi_sigmoid, [], 2 * d, BF16,
                    tm=tm, tn=tn_in, w_resident=True, name=f"in_gate{layer}")

        q = _mm([cq], [W(0, w_q)], _epi_q_heads, rope_ex, heads * HEAD_PAD, BF16,
                tm=tm, tn=_tile(heads * HEAD_PAD, 512, 2 * V7X_LANES), w_resident=False, name=f"q_up{layer}")
        kv = _mm([ckv], [W(0, w_kv)], _epi_id, [], heads * (QK_NOPE + V_HEAD), BF16,
                 tm=tm, tn=_tile(heads * (QK_NOPE + V_HEAD), 512), w_resident=False, name=f"kv_up{layer}")
        y_a = _attention(q, kv, k_rope, batch=batch, seq=seq, heads=heads, name=f"attention{layer}")
        y_b = _gmlp(uv, v_norm_g[layer], v_norm_b[layer], w_sp[layer], b_sp[layer], name=f"gmlp{layer}")

        merged = _mm([y_a, y_b], [W(0, w_branch_a, layer), W(1, w_branch_b, layer)],
                     _epi_gated_merge, [(gates, "mn", 0), (gates, "mn", d // tn_d)], d, BF16,
                     tm=tm, tn=tn_d, w_resident=True, name=f"merge{layer}")
        xf = _mm([merged], [W(0, w_out, layer)], _epi_residual, [(xf, "mn", 0)], d, F32,
                 tm=tm, tn=tn_d, w_resident=True, name=f"mix_out{layer}")

        i = layer // 2
        last = layer == depth - 1
        if layer % 2 == 0:
            ff = dense_w1.shape[2]
            h = _rmsnorm(xf, ffn_norm[layer], BF16, name=f"ffn_norm{layer}")
            act = _mm([h], [W(0, dense_w1, i), W(0, dense_w3, i)], _epi_swiglu, [], ff, BF16,
                      tm=tm, tn=_tile(ff, 256), w_resident=True, name=f"dense_up{layer}")
            xf = _mm([act], [W(0, dense_w2[i].astype(BF16))], _epi_residual, [(xf, "mn", 0)], d, F32,
                     tm=_tile(t, 512, 8), tn=_tile(d, 256), w_resident=False, name=f"dense_down{layer}")
            if last:
                out = _rmsnorm(xf, final_norm, F32, name="final_norm")
        else:
            n_experts = w_router.shape[2]
            fe = moe_w1.shape[3]
            hf, route = _norm_router(xf, ffn_norm[layer], w_router[i], name=f"ffn_norm_router{layer}")
            slot_tok, slot_of_assign, block_exp = _routing_tables(route, n_experts)
            xb = _gather_rows(hf, slot_tok, name=f"moe_gather{layer}")
            act = _mm([xb], [W(0, moe_w1, i), W(0, moe_w3, i)], _epi_swiglu, [], fe, BF16,
                      tm=EXPERT_BLOCK, tn=_tile(fe, 512), w_resident=True, group=block_exp, name=f"moe_up{layer}")
            yb = _mm([act], [W(0, moe_w2, i)], _epi_id, [], d, F32,
                     tm=EXPERT_BLOCK, tn=tn_d, w_resident=True, group=block_exp, name=f"moe_down{layer}")
            xf = _combine(xf, route, slot_of_assign, yb, final_norm if last else None, name=f"moe_combine{layer}")
            if last:
                out = xf
    return out.reshape(batch, seq, d)
```

```python
import functools
import math
from typing import Any, NamedTuple

import jax
import jax.numpy as jnp
from jax import lax
from jax.experimental import pallas as pl
from jax.experimental.pallas import tpu as pltpu

F32 = jnp.float32
BF16 = jnp.bfloat16

V7X_LANES = 128
V7X_VMEM_BYTES = 64 * 1024 * 1024
V7X_VMEM_CAP = V7X_VMEM_BYTES - 6 * 1024 * 1024

QK_NOPE = 128
QK_ROPE = 64
V_HEAD = 128
HEAD_PAD = 256
GROUP_DIM = 128
CHUNK = 128
TOP_K = 2
EXPERT_BLOCK = 512
NORM_EPS = 1e-6
ROPE_THETA = 10000.0
Q_PRESCALE = float(QK_NOPE + QK_ROPE) ** -0.5 * math.log2(math.e)


def _tile(n, pref, unit=V7X_LANES):
    t = (min(pref, n) // unit) * unit
    while t >= unit:
        if n % t == 0:
            return t
        t -= unit
    return n


def _vmem_limit(block_bytes, temp_bytes):
    return int(min(V7X_VMEM_CAP, 2 * block_bytes + temp_bytes + (4 << 20)))


def _nbytes(shape, dtype):
    n = 1
    for s in shape:
        n *= s
    return n * jnp.dtype(dtype).itemsize


class W(NamedTuple):
    xi: int
    arr: Any
    lead: Any = None
    off: int = 0
    nt: bool = False
    row0: int = 0


_NT_DIMS = (((1,), (1,)), ((), ()))


def _mm_body(*refs, npf, nx, ws, ne, epi, w_resident, grouped):
    pf = refs[:npf]
    refs = refs[npf:]
    x_refs = refs[:nx]
    w_refs = refs[nx:nx + len(ws)]
    e_refs = refs[nx + len(ws):nx + len(ws) + ne]
    o_ref = refs[nx + len(ws) + ne]
    wbf_refs = refs[nx + len(ws) + ne + 1:]

    if w_resident:
        i = pl.program_id(1)
        first = i == 0
        if grouped:
            g_ref = pf[0]
            first = jnp.logical_or(first, g_ref[i] != g_ref[jnp.maximum(i - 1, 0)])

        @pl.when(first)
        def _():
            for wr, wbf in zip(w_refs, wbf_refs):
                wbf[...] = wr[...].astype(BF16)

        mats = [wbf[...] for wbf in wbf_refs]
    else:
        mats = [wr[...].astype(BF16) for wr in w_refs]
    accs = []
    for w, mat in zip(ws, mats):
        x = x_refs[w.xi][...]
        if w.nt:
            accs.append(lax.dot_general(x, mat, _NT_DIMS, preferred_element_type=F32))
        else:
            accs.append(jnp.dot(x, mat, preferred_element_type=F32))
    o_ref[...] = epi(accs, [e[...] for e in e_refs]).astype(o_ref.dtype)


def _mm(xs, ws, epi, extras, n, out_dtype, *, tm, tn, w_resident, group=None, name):
    m = xs[0].shape[0]
    gm, gn = m // tm, n // tn
    npf = 0 if group is None else 1
    if w_resident:
        grid = (gn, gm)
        ij = lambda a, b: (b, a)
    else:
        grid = (gm, gn)
        ij = lambda a, b: (a, b)

    def imap(f):
        return lambda a, b, *pf: f(*ij(a, b), *pf)

    in_specs = []
    blk = 0
    for x in xs:
        in_specs.append(pl.BlockSpec((tm, x.shape[1]), imap(lambda i, j, *pf: (i, 0))))
        blk += _nbytes((tm, x.shape[1]), x.dtype)
    scratch = []
    for w in ws:
        lead = () if w.lead is None else (w.lead,)
        none = (None,) * len(lead)
        if w.nt:
            assert group is None and w.arr.ndim == 3 and not w_resident
            k = w.arr.shape[-1]
            tile = (tn, k)
            in_specs.append(pl.BlockSpec((pl.Squeezed(), pl.Element(tn), pl.Element(k)), imap(
                lambda i, j, *pf, lead=w.lead, row0=w.row0: (lead, pl.multiple_of(row0 + j * tn, 8), 0))))
        elif group is None:
            assert w.arr.ndim == 2 + len(lead)
            k = w.arr.shape[-2]
            tile = (k, tn)
            in_specs.append(pl.BlockSpec(none + tile, imap(
                lambda i, j, *pf, lead=lead, off=w.off: lead + (0, j + off))))
        else:
            assert w.arr.ndim == 3 + len(lead)
            k = w.arr.shape[-2]
            tile = (k, tn)
            in_specs.append(pl.BlockSpec(none + (None,) + tile, imap(
                lambda i, j, g, lead=lead: lead + (g[i], 0, j))))
        blk += _nbytes(tile, w.arr.dtype)
        if w_resident:
            scratch.append(pltpu.VMEM(tile, BF16))
    for arr, kind, off in extras:
        if kind == "mn":
            in_specs.append(pl.BlockSpec((tm, tn), imap(lambda i, j, *pf, off=off: (i, j + off))))
            blk += _nbytes((tm, tn), arr.dtype)
        else:
            in_specs.append(pl.BlockSpec((tm, arr.shape[1]), imap(lambda i, j, *pf: (i, 0))))
            blk += _nbytes((tm, arr.shape[1]), arr.dtype)
    out_spec = pl.BlockSpec((tm, tn), imap(lambda i, j, *pf: (i, j)))
    blk += _nbytes((tm, tn), out_dtype)
    temp = (len(ws) + 2) * tm * tn * 4 + sum(_nbytes(s.shape, BF16) for s in scratch)
    if not w_resident:
        temp += sum(_nbytes((w.arr.shape[-2] if not w.nt else w.arr.shape[-1], tn), BF16)
                    for w in ws if w.arr.dtype != BF16)
    body = functools.partial(_mm_body, npf=npf, nx=len(xs), ws=tuple(w._replace(arr=None) for w in ws),
                             ne=len(extras), epi=epi, w_resident=w_resident, grouped=group is not None)
    call = pl.pallas_call(
        body,
        grid_spec=pltpu.PrefetchScalarGridSpec(
            num_scalar_prefetch=npf, grid=grid, in_specs=in_specs, out_specs=out_spec, scratch_shapes=scratch),
        out_shape=jax.ShapeDtypeStruct((m, n), out_dtype),
        compiler_params=pltpu.CompilerParams(
            dimension_semantics=("arbitrary", "arbitrary"), vmem_limit_bytes=_vmem_limit(blk, temp)),
        name=name,
    )
    args = [] if group is None else [group]
    args += list(xs) + [w.arr for w in ws] + [a for a, _, _ in extras]
    return call(*args)


def _epi_id(accs, ex):
    return accs[0]


def _epi_gelu(accs, ex):
    a = accs[0]
    return 0.5 * a * (1.0 + lax.erf(a * (2.0 ** -0.5)))


def _epi_sigmoid(accs, ex):
    return jax.nn.sigmoid(accs[0])


def _epi_residual(accs, ex):
    return ex[0] + accs[0]


def _epi_swiglu(accs, ex):
    return jax.nn.silu(accs[0]) * accs[1]


def _epi_gated_merge(accs, ex):
    return ex[0].astype(F32) * accs[0] + ex[1].astype(F32) * accs[1]


def _rope_lanes(t, cos_t, sin_a, sin_b):
    half = QK_ROPE // 2
    return (t * cos_t + pltpu.roll(t, half, 1) * sin_a
            + pltpu.roll(t, V7X_LANES - half, 1) * sin_b)


def _epi_rope_all(accs, ex):
    return _rope_lanes(accs[0], *ex)


def _epi_q_heads(accs, ex):
    a = accs[0]
    outs = []
    for g in range(a.shape[1] // V7X_LANES):
        t = a[:, g * V7X_LANES:(g + 1) * V7X_LANES]
        outs.append((_rope_lanes(t, *ex) if g % 2 else t) * Q_PRESCALE)
    return jnp.concatenate(outs, axis=1)


def _rmsnorm_body(x_ref, g_ref, o_ref):
    x = x_ref[...].astype(F32)
    y = x * lax.rsqrt(jnp.mean(x * x, axis=-1, keepdims=True) + NORM_EPS)
    o_ref[...] = (y * g_ref[...]).astype(o_ref.dtype)


def _rmsnorm(x, g, out_dtype, *, width=None, col_block=0, name):
    m = x.shape[0]
    width = x.shape[1] if width is None else width
    tm = _tile(m, 256, 8)
    blk = _nbytes((tm, width), x.dtype) + _nbytes((tm, width), out_dtype)
    return pl.pallas_call(
        _rmsnorm_body,
        grid=(m // tm,),
        in_specs=[pl.BlockSpec((tm, width), lambda i: (i, col_block)),
                  pl.BlockSpec((1, width), lambda i: (0, 0))],
        out_specs=pl.BlockSpec((tm, width), lambda i: (i, 0)),
        out_shape=jax.ShapeDtypeStruct((m, width), out_dtype),
        compiler_params=pltpu.CompilerParams(
            dimension_semantics=("arbitrary",), vmem_limit_bytes=_vmem_limit(blk, 3 * tm * width * 4)),
        name=name,
    )(x, g.reshape(1, width).astype(F32))


def _norm_router_body(x_ref, g_ref, wr_ref, h_ref, route_ref, *, n_experts):
    x = x_ref[...]
    h = x * lax.rsqrt(jnp.mean(x * x, axis=-1, keepdims=True) + NORM_EPS) * g_ref[...]
    h_ref[...] = h
    logits = jnp.dot(h, wr_ref[...], preferred_element_type=F32, precision=lax.Precision.HIGHEST)
    lane = lax.broadcasted_iota(jnp.int32, logits.shape, 1).astype(F32)
    neg = jnp.float32(-jnp.inf)
    far = jnp.float32(V7X_LANES)
    l1 = jnp.where(lane < n_experts, logits, neg)
    m1 = jnp.max(l1, axis=-1, keepdims=True)
    i1 = jnp.min(jnp.where(l1 == m1, lane, far), axis=-1, keepdims=True)
    l2 = jnp.where(lane == i1, neg, l1)
    m2 = jnp.max(l2, axis=-1, keepdims=True)
    i2 = jnp.min(jnp.where(l2 == m2, lane, far), axis=-1, keepdims=True)
    e = jnp.exp(m2 - m1)
    g1 = 1.0 / (1.0 + e)
    g2 = e / (1.0 + e)
    route = jnp.where(lane == 0, i1, jnp.where(lane == 1, i2, jnp.where(lane == 2, g1, jnp.where(lane == 3, g2, 0.0))))
    route_ref[...] = route


def _norm_router(x, g, w_router, *, name):
    m, d = x.shape
    n_experts = w_router.shape[1]
    wr = jnp.pad(w_router.astype(F32), ((0, 0), (0, V7X_LANES - n_experts)))
    tm = _tile(m, 256, 8)
    blk = 2 * _nbytes((tm, d), F32) + _nbytes((d, V7X_LANES), F32) + _nbytes((tm, V7X_LANES), F32)
    return pl.pallas_call(
        functools.partial(_norm_router_body, n_experts=n_experts),
        grid=(m // tm,),
        in_specs=[pl.BlockSpec((tm, d), lambda i: (i, 0)),
                  pl.BlockSpec((1, d), lambda i: (0, 0)),
                  pl.BlockSpec((d, V7X_LANES), lambda i: (0, 0))],
        out_specs=[pl.BlockSpec((tm, d), lambda i: (i, 0)),
                   pl.BlockSpec((tm, V7X_LANES), lambda i: (i, 0))],
        out_shape=[jax.ShapeDtypeStruct((m, d), F32), jax.ShapeDtypeStruct((m, V7X_LANES), F32)],
        compiler_params=pltpu.CompilerParams(
            dimension_semantics=("arbitrary",), vmem_limit_bytes=_vmem_limit(blk, 4 * tm * d * 4)),
        name=name,
    )(x, g.reshape(1, d).astype(F32), wr)


def _attn_body(q_ref, kn_ref, kr_ref, v_ref, o_ref, kcat_ref, *, sub):
    @pl.when(pl.program_id(2) == 0)
    def _():
        kcat_ref[:, :QK_NOPE] = kn_ref[...]
        kcat_ref[:, QK_NOPE:] = kr_ref[...]

    for r0 in range(0, q_ref.shape[0], sub):
        rows = slice(r0, r0 + sub)
        s = lax.dot_general(q_ref[rows, :], kcat_ref[...], (((1,), (1,)), ((), ())),
                            preferred_element_type=F32)
        p = jnp.exp2(s - jnp.max(s, axis=-1, keepdims=True))
        l = jnp.sum(p, axis=-1, keepdims=True)
        o = jnp.dot(p.astype(BF16), v_ref[...], preferred_element_type=F32)
        o_ref[rows, :] = (o / l).astype(o_ref.dtype)


def _attention(q, kv, k_rope, *, batch, seq, heads, name):
    t = batch * seq
    sub = _tile(seq, 256, 8)
    tq = _tile(seq, 8 * sub, sub)
    nq = seq // tq
    blk = (_nbytes((tq, HEAD_PAD), BF16) + 3 * _nbytes((seq, V7X_LANES), BF16) + _nbytes((tq, V_HEAD), BF16))
    return pl.pallas_call(
        functools.partial(_attn_body, sub=sub),
        grid=(batch, heads, nq),
        in_specs=[pl.BlockSpec((tq, HEAD_PAD), lambda b, h, i: (b * nq + i, h)),
                  pl.BlockSpec((seq, QK_NOPE), lambda b, h, i: (b, 2 * h)),
                  pl.BlockSpec((seq, V7X_LANES), lambda b, h, i: (b, 0)),
                  pl.BlockSpec((seq, V_HEAD), lambda b, h, i: (b, 2 * h + 1))],
        out_specs=pl.BlockSpec((tq, V_HEAD), lambda b, h, i: (b * nq + i, h)),
        out_shape=jax.ShapeDtypeStruct((t, heads * V_HEAD), BF16),
        scratch_shapes=[pltpu.VMEM((seq, HEAD_PAD), BF16)],
        compiler_params=pltpu.CompilerParams(
            dimension_semantics=("arbitrary", "arbitrary", "arbitrary"),
            vmem_limit_bytes=_vmem_limit(blk, _nbytes((seq, HEAD_PAD), BF16) + 4 * tq * seq * 4)),
        name=name,
    )(q, kv, k_rope, kv)


def _gmlp_body(u_ref, v_ref, g_ref, b_ref, w_ref, bs_ref, o_ref, *, n_chunks, n_groups):
    v = v_ref[...].astype(F32)
    mu = jnp.mean(v, axis=-1, keepdims=True)
    vc = v - mu
    vn = vc * lax.rsqrt(jnp.mean(vc * vc, axis=-1, keepdims=True) + NORM_EPS)
    vn = (vn * g_ref[...] + b_ref[...]).astype(BF16)
    for c in range(n_chunks):
        rows = slice(c * CHUNK, (c + 1) * CHUNK)
        for g in range(n_groups):
            cols = slice(g * GROUP_DIM, (g + 1) * GROUP_DIM)
            s = jnp.dot(w_ref[g], vn[rows, cols], preferred_element_type=F32) + bs_ref[g]
            o_ref[rows, cols] = (u_ref[rows, cols].astype(F32) * s).astype(o_ref.dtype)


def _gmlp(uv, v_norm_g, v_norm_b, w_sp, b_sp, *, name):
    t = uv.shape[0]
    gw = uv.shape[1] // 2
    n_groups = w_sp.shape[0]
    rows = _tile(t, 2 * CHUNK, CHUNK)
    b_full = jnp.broadcast_to(b_sp.astype(F32)[:, :, None], (n_groups, CHUNK, GROUP_DIM))
    blk = 3 * _nbytes((rows, gw), BF16) + _nbytes(w_sp.shape, BF16) + _nbytes(b_full.shape, F32)
    return pl.pallas_call(
        functools.partial(_gmlp_body, n_chunks=rows // CHUNK, n_groups=n_groups),
        grid=(t // rows,),
        in_specs=[pl.BlockSpec((rows, gw), lambda i: (i, 0)),
                  pl.BlockSpec((rows, gw), lambda i: (i, 1)),
                  pl.BlockSpec((1, gw), lambda i: (0, 0)),
                  pl.BlockSpec((1, gw), lambda i: (0, 0)),
                  pl.BlockSpec((n_groups, CHUNK, CHUNK), lambda i: (0, 0, 0)),
                  pl.BlockSpec((n_groups, CHUNK, GROUP_DIM), lambda i: (0, 0, 0))],
        out_specs=pl.BlockSpec((rows, gw), lambda i: (i, 0)),
        out_shape=jax.ShapeDtypeStruct((t, gw), BF16),
        compiler_params=pltpu.CompilerParams(
            dimension_semantics=("arbitrary",), vmem_limit_bytes=_vmem_limit(blk, 4 * rows * gw * 4)),
        name=name,
    )(uv, uv, v_norm_g.reshape(1, gw).astype(F32), v_norm_b.reshape(1, gw).astype(F32),
      w_sp.astype(BF16), b_full)


def _row_copy(src_hbm, dst_vmem, src_row, dst_row, sem):
    return pltpu.make_async_copy(src_hbm.at[pl.ds(src_row, 1)], dst_vmem.at[pl.ds(dst_row, 1)], sem)


def _gather_body(tok_ref, h_hbm, o_ref, buf, sem, *, rows, n_blocks):
    r = pl.program_id(0)

    def issue(block, slot):
        base = block * rows

        def f(k, c):
            _row_copy(h_hbm, buf.at[slot], tok_ref[base + k], k, sem.at[slot]).start()
            return c

        lax.fori_loop(0, rows, f, 0, unroll=8)

    @pl.when(r == 0)
    def _():
        issue(0, 0)

    @pl.when(r + 1 < n_blocks)
    def _():
        issue(r + 1, (r + 1) % 2)

    slot = r % 2

    def drain(k, c):
        _row_copy(h_hbm, buf.at[slot], 0, k, sem.at[slot]).wait()
        return c

    lax.fori_loop(0, rows, drain, 0, unroll=8)
    o_ref[...] = buf[slot].astype(o_ref.dtype)


def _gather_rows(h, slot_tok, *, name):
    n_slots = slot_tok.shape[0]
    d = h.shape[1]
    rows = EXPERT_BLOCK
    n_blocks = n_slots // rows
    blk = _nbytes((rows, d), BF16)
    return pl.pallas_call(
        functools.partial(_gather_body, rows=rows, n_blocks=n_blocks),
        grid_spec=pltpu.PrefetchScalarGridSpec(
            num_scalar_prefetch=1, grid=(n_blocks,),
            in_specs=[pl.BlockSpec(memory_space=pl.ANY)],
            out_specs=pl.BlockSpec((rows, d), lambda r, tok: (r, 0)),
            scratch_shapes=[pltpu.VMEM((2, rows, d), F32), pltpu.SemaphoreType.DMA((2,))]),
        out_shape=jax.ShapeDtypeStruct((n_slots, d), BF16),
        compiler_params=pltpu.CompilerParams(
            dimension_semantics=("arbitrary",),
            vmem_limit_bytes=_vmem_limit(blk, 3 * _nbytes((rows, d), F32))),
        name=name,
    )(slot_tok, h)


def _combine_body(slot_ref, x_ref, route_ref, g_ref, y_hbm, o_ref, buf, sem, *, rows, n_tiles, norm):
    i = pl.program_id(0)

    def issue(tile, slot):
        base = tile * rows

        def f(k, c):
            for j in range(TOP_K):
                _row_copy(y_hbm, buf.at[slot, j], slot_ref[(base + k) * TOP_K + j], k, sem.at[slot]).start()
            return c

        lax.fori_loop(0, rows, f, 0, unroll=4)

    @pl.when(i == 0)
    def _():
        issue(0, 0)

    @pl.when(i + 1 < n_tiles)
    def _():
        issue(i + 1, (i + 1) % 2)

    slot = i % 2

    def drain(k, c):
        for j in range(TOP_K):
            _row_copy(y_hbm, buf.at[slot, j], 0, k, sem.at[slot]).wait()
        return c

    lax.fori_loop(0, rows, drain, 0, unroll=4)
    route = route_ref[...]
    y = x_ref[...] + (buf[slot, 0] * route[:, 2:3] + buf[slot, 1] * route[:, 3:4])
    if norm:
        y = y * lax.rsqrt(jnp.mean(y * y, axis=-1, keepdims=True) + NORM_EPS) * g_ref[...]
    o_ref[...] = y


def _combine(x, route, slot_of_assign, yb, norm_gain, *, name):
    t, d = x.shape
    rows = _tile(t, 256, 8)
    n_tiles = t // rows
    norm = norm_gain is not None
    g = (norm_gain if norm else jnp.ones((d,), F32)).reshape(1, d).astype(F32)
    blk = 2 * _nbytes((rows, d), F32) + _nbytes((rows, V7X_LANES), F32)
    return pl.pallas_call(
        functools.partial(_combine_body, rows=rows, n_tiles=n_tiles, norm=norm),
        grid_spec=pltpu.PrefetchScalarGridSpec(
            num_scalar_prefetch=1, grid=(n_tiles,),
            in_specs=[pl.BlockSpec((rows, d), lambda i, s: (i, 0)),
                      pl.BlockSpec((rows, V7X_LANES), lambda i, s: (i, 0)),
                      pl.BlockSpec((1, d), lambda i, s: (0, 0)),
                      pl.BlockSpec(memory_space=pl.ANY)],
            out_specs=pl.BlockSpec((rows, d), lambda i, s: (i, 0)),
            scratch_shapes=[pltpu.VMEM((2, TOP_K, rows, d), F32), pltpu.SemaphoreType.DMA((2,))]),
        out_shape=jax.ShapeDtypeStruct((t, d), F32),
        compiler_params=pltpu.CompilerParams(
            dimension_semantics=("arbitrary",),
            vmem_limit_bytes=_vmem_limit(blk, (2 * TOP_K + 3) * _nbytes((rows, d), F32))),
        name=name,
    )(slot_of_assign, x, route, g, yb)


def _routing_tables(route, n_experts):
    t = route.shape[0]
    a = t * TOP_K
    flat_e = route[:, :TOP_K].astype(jnp.int32).reshape(a)
    flat_tok = jnp.arange(a, dtype=jnp.int32) // TOP_K
    order = jnp.argsort(flat_e)
    se = flat_e[order]
    counts = jnp.zeros((n_experts,), jnp.int32).at[flat_e].add(1)
    padded = (counts + EXPERT_BLOCK - 1) // EXPERT_BLOCK * EXPERT_BLOCK
    pad_end = jnp.cumsum(padded)
    pad_start = pad_end - padded
    grp_start = jnp.cumsum(counts) - counts
    dest = pad_start[se] + (jnp.arange(a, dtype=jnp.int32) - grp_start[se])
    n_blocks = -(-a // EXPERT_BLOCK) + n_experts
    n_slots = n_blocks * EXPERT_BLOCK
    slot_tok = jnp.zeros((n_slots,), jnp.int32).at[dest].set(flat_tok[order])
    slot_of_assign = jnp.zeros((a,), jnp.int32).at[order].set(dest.astype(jnp.int32))
    block_start = jnp.arange(n_blocks, dtype=jnp.int32) * EXPERT_BLOCK
    block_exp = jnp.minimum(jnp.searchsorted(pad_end, block_start, side="right"),
                            n_experts - 1).astype(jnp.int32)
    return slot_tok, slot_of_assign, block_exp


def _rope_panels(positions):
    inv_freq = ROPE_THETA ** (-jnp.arange(0, QK_ROPE, 2, dtype=F32) / QK_ROPE)
    ang = positions.astype(F32).reshape(-1, 1) * inv_freq
    cos, sin = jnp.cos(ang), jnp.sin(ang)
    z = jnp.zeros_like(cos)
    pad = jnp.zeros((cos.shape[0], V7X_LANES - QK_ROPE), F32)
    cos_t = jnp.concatenate([cos, cos, pad], axis=1)
    sin_a = jnp.concatenate([z, sin, pad], axis=1)
    sin_b = jnp.concatenate([-sin, z, pad], axis=1)
    return cos_t, sin_a, sin_b


def kernel(x, positions, mix_norm, w_in, q_norm, kv_norm, w_uq, w_ukv, v_norm_g, v_norm_b, w_sp, b_sp, w_branch_a, w_branch_b, w_out, ffn_norm, dense_w1, dense_w3, dense_w2, w_router, moe_w1, moe_w3, moe_w2, final_norm):
    batch, seq, d = x.shape
    t = batch * seq
    depth = mix_norm.shape[0]
    ql, kvl = q_norm.shape[1], kv_norm.shape[1]
    heads = w_ukv.shape[2] // (QK_NOPE + V_HEAD)
    gw = v_norm_g.shape[1]
    o2 = ql + kvl
    o3 = o2 + QK_ROPE
    o4 = o3 + 2 * gw
    assert ql % kvl == 0
    tm = _tile(t, 1024, 8)
    tn_d = _tile(d, 512)
    w_in_t = jnp.swapaxes(w_in, 1, 2)

    rope = _rope_panels(positions)
    rope_ex = [(p, "m", 0) for p in rope]
    xf = x.reshape(t, d)
    out = None
    for layer in range(depth):
        w_q = jnp.pad(w_uq[layer].reshape(ql, heads, QK_NOPE + QK_ROPE),
                      ((0, 0), (0, 0), (0, HEAD_PAD - QK_NOPE - QK_ROPE))).reshape(ql, heads * HEAD_PAD).astype(BF16)
        w_kv = w_ukv[layer].astype(BF16)

        h = _rmsnorm(xf, mix_norm[layer], BF16, name=f"mix_norm{layer}")
        cqkv = _mm([h], [W(0, w_in_t, layer, nt=True)], _epi_id, [], o2, F32,
                   tm=tm, tn=_tile(o2, 512), w_resident=False, name=f"in_latent{layer}")
        cq = _rmsnorm(cqkv, q_norm[layer], BF16, width=ql, col_block=0, name=f"q_norm{layer}")
        ckv = _rmsnorm(cqkv, kv_norm[layer], BF16, width=kvl, col_block=ql // kvl, name=f"kv_norm{layer}")
        k_rope = _mm([h], [W(0, w_in_t, layer, nt=True, row0=o2)], _epi_rope_all, rope_ex, V7X_LANES, BF16,
                     tm=tm, tn=V7X_LANES, w_resident=False, name=f"in_krope{layer}")
        uv = _mm([h], [W(0, w_in_t, layer, nt=True, row0=o3)], _epi_gelu, [], 2 * gw, BF16,
                 tm=tm, tn=_tile(2 * gw, 512), w_resident=False, name=f"in_uv{layer}")
        gates = _mm([h], [W(0, w_in_t, layer, nt=True, row0=o4)], _epi_sigmoid, [], 2 * d, BF16,
                    tm=tm, tn=_tile(2 * d, 512), w_resident=False, name=f"in_gate{layer}")

        q = _mm([cq], [W(0, w_q)], _epi_q_heads, rope_ex, heads * HEAD_PAD, BF16,
                tm=tm, tn=_tile(heads * HEAD_PAD, 512, 2 * V7X_LANES), w_resident=False, name=f"q_up{layer}")
        kv = _mm([ckv], [W(0, w_kv)], _epi_id, [], heads * (QK_NOPE + V_HEAD), BF16,
                 tm=tm, tn=_tile(heads * (QK_NOPE + V_HEAD), 512), w_resident=False, name=f"kv_up{layer}")
        y_a = _attention(q, kv, k_rope, batch=batch, seq=seq, heads=heads, name=f"attention{layer}")
        y_b = _gmlp(uv, v_norm_g[layer], v_norm_b[layer], w_sp[layer], b_sp[layer], name=f"gmlp{layer}")

        merged = _mm([y_a, y_b], [W(0, w_branch_a, layer), W(1, w_branch_b, layer)],
                     _epi_gated_merge, [(gates, "mn", 0), (gates, "mn", d // tn_d)], d, BF16,
                     tm=tm, tn=tn_d, w_resident=False, name=f"merge{layer}")
        xf = _mm([merged], [W(0, w_out, layer)], _epi_residual, [(xf, "mn", 0)], d, F32,
                 tm=tm, tn=tn_d, w_resident=False, name=f"mix_out{layer}")

        i = layer // 2
        last = layer == depth - 1
        if layer % 2 == 0:
            ff = dense_w1.shape[2]
            h = _rmsnorm(xf, ffn_norm[layer], BF16, name=f"ffn_norm{layer}")
            act = _mm([h], [W(0, dense_w1, i), W(0, dense_w3, i)], _epi_swiglu, [], ff, BF16,
                      tm=tm, tn=_tile(ff, 256), w_resident=False, name=f"dense_up{layer}")
            xf = _mm([act], [W(0, dense_w2[i].astype(BF16))], _epi_residual, [(xf, "mn", 0)], d, F32,
                     tm=_tile(t, 512, 8), tn=_tile(d, 256), w_resident=False, name=f"dense_down{layer}")
            if last:
                out = _rmsnorm(xf, final_norm, F32, name="final_norm")
        else:
            n_experts = w_router.shape[2]
            fe = moe_w1.shape[3]
            hf, route = _norm_router(xf, ffn_norm[layer], w_router[i], name=f"ffn_norm_router{layer}")
            slot_tok, slot_of_assign, block_exp = _routing_tables(route, n_experts)
            xb = _gather_rows(hf, slot_tok, name=f"moe_gather{layer}")
            act = _mm([xb], [W(0, moe_w1, i), W(0, moe_w3, i)], _epi_swiglu, [], fe, BF16,
                      tm=EXPERT_BLOCK, tn=_tile(fe, 512), w_resident=True, group=block_exp, name=f"moe_up{layer}")
            yb = _mm([act], [W(0, moe_w2, i)], _epi_id, [], d, F32,
                     tm=EXPERT_BLOCK, tn=tn_d, w_resident=True, group=block_exp, name=f"moe_down{layer}")
            xf = _combine(xf, route, slot_of_assign, yb, final_norm if last else None, name=f"moe_combine{layer}")
            if last:
                out = xf
    return out.reshape(batch, seq, d)
```

```python
import functools
import math
from typing import Any, NamedTuple

import jax
import jax.numpy as jnp
from jax import lax
from jax.experimental import pallas as pl
from jax.experimental.pallas import tpu as pltpu

F32 = jnp.float32
BF16 = jnp.bfloat16

V7X_LANES = 128
V7X_VMEM_BYTES = 64 * 1024 * 1024
V7X_VMEM_CAP = V7X_VMEM_BYTES - 6 * 1024 * 1024

QK_NOPE = 128
QK_ROPE = 64
V_HEAD = 128
HEAD_PAD = 256
GROUP_DIM = 128
CHUNK = 128
TOP_K = 2
MOE_TILE = 1024
NORM_EPS = 1e-6
ROPE_THETA = 10000.0
Q_PRESCALE = float(QK_NOPE + QK_ROPE) ** -0.5 * math.log2(math.e)


def _tile(n, pref, unit=V7X_LANES):
    t = (min(pref, n) // unit) * unit
    while t >= unit:
        if n % t == 0:
            return t
        t -= unit
    return n


def _vmem_limit(block_bytes, temp_bytes):
    return int(min(V7X_VMEM_CAP, 2 * block_bytes + temp_bytes + (4 << 20)))


def _nbytes(shape, dtype):
    n = 1
    for s in shape:
        n *= s
    return n * jnp.dtype(dtype).itemsize


class W(NamedTuple):
    xi: int
    arr: Any
    lead: Any = None
    off: int = 0
    nt: bool = False
    row0: int = 0


_NT_DIMS = (((1,), (1,)), ((), ()))


def _mm_compute(x_refs, w_refs, e_refs, o_ref, ws, epi):
    accs = []
    for w, wr in zip(ws, w_refs):
        x = x_refs[w.xi][...]
        mat = wr[...].astype(BF16)
        if w.nt:
            accs.append(lax.dot_general(x, mat, _NT_DIMS, preferred_element_type=F32))
        else:
            accs.append(jnp.dot(x, mat, preferred_element_type=F32))
    o_ref[...] = epi(accs, [e[...] for e in e_refs]).astype(o_ref.dtype)


def _mm_body(*refs, npf, nx, ws, ne, epi):
    pf = refs[:npf]
    refs = refs[npf:]
    x_refs = refs[:nx]
    w_refs = refs[nx:nx + len(ws)]
    e_refs = refs[nx + len(ws):nx + len(ws) + ne]
    o_ref = refs[nx + len(ws) + ne]
    if npf:
        active = pl.program_id(0) < pf[1][0]
        pl.when(active)(lambda: _mm_compute(x_refs, w_refs, e_refs, o_ref, ws, epi))

        @pl.when(jnp.logical_not(active))
        def _():
            o_ref[...] = jnp.zeros_like(o_ref)
    else:
        _mm_compute(x_refs, w_refs, e_refs, o_ref, ws, epi)


def _active_ij(i, j, n_active, gn):
    return jnp.minimum(i, n_active - 1), jnp.where(i < n_active, j, gn - 1)


def _mm(xs, ws, epi, extras, n, out_dtype, *, tm, tn, group=None, name):
    m = xs[0].shape[0]
    gm, gn = m // tm, n // tn
    npf = 0 if group is None else 2

    def imap(f):
        if group is None:
            return f
        return lambda i, j, g, na: f(*_active_ij(i, j, na[0], gn), g)

    in_specs = []
    blk = 0
    for x in xs:
        in_specs.append(pl.BlockSpec((tm, x.shape[1]), imap(lambda i, j, *pf: (i, 0))))
        blk += _nbytes((tm, x.shape[1]), x.dtype)
    temp = (len(ws) + 2) * tm * tn * 4
    for w in ws:
        lead = () if w.lead is None else (w.lead,)
        none = (None,) * len(lead)
        if w.nt:
            assert group is None and w.arr.ndim == 3
            k = w.arr.shape[-1]
            in_specs.append(pl.BlockSpec((pl.Squeezed(), pl.Element(tn), pl.Element(k)), imap(
                lambda i, j, *pf, lead=w.lead, row0=w.row0: (lead, pl.multiple_of(row0 + j * tn, 8), 0))))
        elif group is None:
            assert w.arr.ndim == 2 + len(lead)
            k = w.arr.shape[-2]
            in_specs.append(pl.BlockSpec(none + (k, tn), imap(
                lambda i, j, *pf, lead=lead, off=w.off: lead + (0, j + off))))
        else:
            assert w.arr.ndim == 3 + len(lead)
            k = w.arr.shape[-2]
            in_specs.append(pl.BlockSpec(none + (None, k, tn), imap(
                lambda i, j, g, lead=lead: lead + (g[i], 0, j))))
        blk += _nbytes((k, tn), w.arr.dtype)
        if w.arr.dtype != BF16:
            temp += _nbytes((k, tn), BF16)
    for arr, kind, off in extras:
        if kind == "mn":
            in_specs.append(pl.BlockSpec((tm, tn), imap(lambda i, j, *pf, off=off: (i, j + off))))
            blk += _nbytes((tm, tn), arr.dtype)
        else:
            in_specs.append(pl.BlockSpec((tm, arr.shape[1]), imap(lambda i, j, *pf: (i, 0))))
            blk += _nbytes((tm, arr.shape[1]), arr.dtype)
    out_spec = pl.BlockSpec((tm, tn), lambda i, j, *pf: (i, j))
    blk += _nbytes((tm, tn), out_dtype)
    body = functools.partial(_mm_body, npf=npf, nx=len(xs), ws=tuple(w._replace(arr=None) for w in ws),
                             ne=len(extras), epi=epi)
    call = pl.pallas_call(
        body,
        grid_spec=pltpu.PrefetchScalarGridSpec(
            num_scalar_prefetch=npf, grid=(gm, gn), in_specs=in_specs, out_specs=out_spec),
        out_shape=jax.ShapeDtypeStruct((m, n), out_dtype),
        compiler_params=pltpu.CompilerParams(
            dimension_semantics=("arbitrary", "arbitrary"), vmem_limit_bytes=_vmem_limit(blk, temp)),
        name=name,
    )
    args = [] if group is None else list(group)
    args += list(xs) + [w.arr for w in ws] + [a for a, _, _ in extras]
    return call(*args)


def _epi_id(accs, ex):
    return accs[0]


def _epi_gelu(accs, ex):
    a = accs[0]
    return 0.5 * a * (1.0 + lax.erf(a * (2.0 ** -0.5)))


def _epi_sigmoid(accs, ex):
    return jax.nn.sigmoid(accs[0])


def _epi_residual(accs, ex):
    return ex[0] + accs[0]


def _epi_swiglu(accs, ex):
    return jax.nn.silu(accs[0]) * accs[1]


def _epi_gated_merge(accs, ex):
    return ex[0].astype(F32) * accs[0] + ex[1].astype(F32) * accs[1]


def _rope_lanes(t, cos_t, sin_a, sin_b):
    half = QK_ROPE // 2
    return (t * cos_t + pltpu.roll(t, half, 1) * sin_a
            + pltpu.roll(t, V7X_LANES - half, 1) * sin_b)


def _epi_rope_all(accs, ex):
    return _rope_lanes(accs[0], *ex)


def _epi_q_heads(accs, ex):
    a = accs[0]
    outs = []
    for g in range(a.shape[1] // V7X_LANES):
        t = a[:, g * V7X_LANES:(g + 1) * V7X_LANES]
        outs.append((_rope_lanes(t, *ex) if g % 2 else t) * Q_PRESCALE)
    return jnp.concatenate(outs, axis=1)


def _rmsnorm_body(x_ref, g_ref, o_ref):
    x = x_ref[...].astype(F32)
    y = x * lax.rsqrt(jnp.mean(x * x, axis=-1, keepdims=True) + NORM_EPS)
    o_ref[...] = (y * g_ref[...]).astype(o_ref.dtype)


def _rmsnorm(x, g, out_dtype, *, width=None, col_block=0, name):
    m = x.shape[0]
    width = x.shape[1] if width is None else width
    tm = _tile(m, 256, 8)
    blk = _nbytes((tm, width), x.dtype) + _nbytes((tm, width), out_dtype)
    return pl.pallas_call(
        _rmsnorm_body,
        grid=(m // tm,),
        in_specs=[pl.BlockSpec((tm, width), lambda i: (i, col_block)),
                  pl.BlockSpec((1, width), lambda i: (0, 0))],
        out_specs=pl.BlockSpec((tm, width), lambda i: (i, 0)),
        out_shape=jax.ShapeDtypeStruct((m, width), out_dtype),
        compiler_params=pltpu.CompilerParams(
            dimension_semantics=("arbitrary",), vmem_limit_bytes=_vmem_limit(blk, 3 * tm * width * 4)),
        name=name,
    )(x, g.reshape(1, width).astype(F32))


def _pack_bf16_pairs(h):
    half = h.shape[1] // 2
    bits = lax.bitcast_convert_type(h.astype(BF16).astype(F32), jnp.uint32)
    return bits[:, :half] | (bits[:, half:] >> 16)


def _unpack_bf16_pairs(p):
    hi = lax.bitcast_convert_type(p & jnp.uint32(0xFFFF0000), F32).astype(BF16)
    lo = lax.bitcast_convert_type(p << 16, F32).astype(BF16)
    return hi, lo


def _norm_router_body(x_ref, g_ref, wr_ref, h_ref, route_ref, *, n_experts):
    x = x_ref[...]
    h = x * lax.rsqrt(jnp.mean(x * x, axis=-1, keepdims=True) + NORM_EPS) * g_ref[...]
    h_ref[...] = _pack_bf16_pairs(h)
    logits = jnp.dot(h, wr_ref[...], preferred_element_type=F32, precision=lax.Precision.HIGHEST)
    lane = lax.broadcasted_iota(jnp.int32, logits.shape, 1).astype(F32)
    neg = jnp.float32(-jnp.inf)
    far = jnp.float32(V7X_LANES)
    l1 = jnp.where(lane < n_experts, logits, neg)
    m1 = jnp.max(l1, axis=-1, keepdims=True)
    i1 = jnp.min(jnp.where(l1 == m1, lane, far), axis=-1, keepdims=True)
    l2 = jnp.where(lane == i1, neg, l1)
    m2 = jnp.max(l2, axis=-1, keepdims=True)
    i2 = jnp.min(jnp.where(l2 == m2, lane, far), axis=-1, keepdims=True)
    e = jnp.exp(m2 - m1)
    g1 = 1.0 / (1.0 + e)
    g2 = e / (1.0 + e)
    route = jnp.where(lane == 0, i1, jnp.where(lane == 1, i2, jnp.where(lane == 2, g1, jnp.where(lane == 3, g2, 0.0))))
    route_ref[...] = route


def _norm_router(x, g, w_router, *, name):
    m, d = x.shape
    n_experts = w_router.shape[1]
    wr = jnp.pad(w_router.astype(F32), ((0, 0), (0, V7X_LANES - n_experts)))
    tm = _tile(m, 256, 8)
    blk = 2 * _nbytes((tm, d), F32) + _nbytes((d, V7X_LANES), F32) + _nbytes((tm, V7X_LANES), F32)
    return pl.pallas_call(
        functools.partial(_norm_router_body, n_experts=n_experts),
        grid=(m // tm,),
        in_specs=[pl.BlockSpec((tm, d), lambda i: (i, 0)),
                  pl.BlockSpec((1, d), lambda i: (0, 0)),
                  pl.BlockSpec((d, V7X_LANES), lambda i: (0, 0))],
        out_specs=[pl.BlockSpec((tm, d // 2), lambda i: (i, 0)),
                   pl.BlockSpec((tm, V7X_LANES), lambda i: (i, 0))],
        out_shape=[jax.ShapeDtypeStruct((m, d // 2), jnp.uint32), jax.ShapeDtypeStruct((m, V7X_LANES), F32)],
        compiler_params=pltpu.CompilerParams(
            dimension_semantics=("arbitrary",), vmem_limit_bytes=_vmem_limit(blk, 4 * tm * d * 4)),
        name=name,
    )(x, g.reshape(1, d).astype(F32), wr)


def _attn_body(q_ref, kn_ref, kr_ref, v_ref, o_ref, kcat_ref, *, sub):
    @pl.when(pl.program_id(2) == 0)
    def _():
        kcat_ref[:, :QK_NOPE] = kn_ref[...]
        kcat_ref[:, QK_NOPE:] = kr_ref[...]

    for r0 in range(0, q_ref.shape[0], sub):
        rows = slice(r0, r0 + sub)
        s = lax.dot_general(q_ref[rows, :], kcat_ref[...], (((1,), (1,)), ((), ())),
                            preferred_element_type=F32)
        p = jnp.exp2(s - jnp.max(s, axis=-1, keepdims=True))
        l = jnp.sum(p, axis=-1, keepdims=True)
        o = jnp.dot(p.astype(BF16), v_ref[...], preferred_element_type=F32)
        o_ref[rows, :] = (o / l).astype(o_ref.dtype)


def _attention(q, kv, k_rope, *, batch, seq, heads, name):
    t = batch * seq
    sub = _tile(seq, 256, 8)
    tq = _tile(seq, 8 * sub, sub)
    nq = seq // tq
    blk = (_nbytes((tq, HEAD_PAD), BF16) + 3 * _nbytes((seq, V7X_LANES), BF16) + _nbytes((tq, V_HEAD), BF16))
    return pl.pallas_call(
        functools.partial(_attn_body, sub=sub),
        grid=(batch, heads, nq),
        in_specs=[pl.BlockSpec((tq, HEAD_PAD), lambda b, h, i: (b * nq + i, h)),
                  pl.BlockSpec((seq, QK_NOPE), lambda b, h, i: (b, 2 * h)),
                  pl.BlockSpec((seq, V7X_LANES), lambda b, h, i: (b, 0)),
                  pl.BlockSpec((seq, V_HEAD), lambda b, h, i: (b, 2 * h + 1))],
        out_specs=pl.BlockSpec((tq, V_HEAD), lambda b, h, i: (b * nq + i, h)),
        out_shape=jax.ShapeDtypeStruct((t, heads * V_HEAD), BF16),
        scratch_shapes=[pltpu.VMEM((seq, HEAD_PAD), BF16)],
        compiler_params=pltpu.CompilerParams(
            dimension_semantics=("arbitrary", "arbitrary", "arbitrary"),
            vmem_limit_bytes=_vmem_limit(blk, _nbytes((seq, HEAD_PAD), BF16) + 4 * tq * seq * 4)),
        name=name,
    )(q, kv, k_rope, kv)


def _gmlp_body(u_ref, v_ref, g_ref, b_ref, w_ref, bs_ref, o_ref, *, n_chunks, n_groups):
    v = v_ref[...].astype(F32)
    mu = jnp.mean(v, axis=-1, keepdims=True)
    vc = v - mu
    vn = vc * lax.rsqrt(jnp.mean(vc * vc, axis=-1, keepdims=True) + NORM_EPS)
    vn = (vn * g_ref[...] + b_ref[...]).astype(BF16)
    for c in range(n_chunks):
        rows = slice(c * CHUNK, (c + 1) * CHUNK)
        for g in range(n_groups):
            cols = slice(g * GROUP_DIM, (g + 1) * GROUP_DIM)
            s = jnp.dot(w_ref[g], vn[rows, cols], preferred_element_type=F32) + bs_ref[g]
            o_ref[rows, cols] = (u_ref[rows, cols].astype(F32) * s).astype(o_ref.dtype)


def _gmlp(uv, v_norm_g, v_norm_b, w_sp, b_sp, *, name):
    t = uv.shape[0]
    gw = uv.shape[1] // 2
    n_groups = w_sp.shape[0]
    rows = _tile(t, 2 * CHUNK, CHUNK)
    b_full = jnp.broadcast_to(b_sp.astype(F32)[:, :, None], (n_groups, CHUNK, GROUP_DIM))
    blk = 3 * _nbytes((rows, gw), BF16) + _nbytes(w_sp.shape, BF16) + _nbytes(b_full.shape, F32)
    return pl.pallas_call(
        functools.partial(_gmlp_body, n_chunks=rows // CHUNK, n_groups=n_groups),
        grid=(t // rows,),
        in_specs=[pl.BlockSpec((rows, gw), lambda i: (i, 0)),
                  pl.BlockSpec((rows, gw), lambda i: (i, 1)),
                  pl.BlockSpec((1, gw), lambda i: (0, 0)),
                  pl.BlockSpec((1, gw), lambda i: (0, 0)),
                  pl.BlockSpec((n_groups, CHUNK, CHUNK), lambda i: (0, 0, 0)),
                  pl.BlockSpec((n_groups, CHUNK, GROUP_DIM), lambda i: (0, 0, 0))],
        out_specs=pl.BlockSpec((rows, gw), lambda i: (i, 0)),
        out_shape=jax.ShapeDtypeStruct((t, gw), BF16),
        compiler_params=pltpu.CompilerParams(
            dimension_semantics=("arbitrary",), vmem_limit_bytes=_vmem_limit(blk, 4 * rows * gw * 4)),
        name=name,
    )(uv, uv, v_norm_g.reshape(1, gw).astype(F32), v_norm_b.reshape(1, gw).astype(F32),
      w_sp.astype(BF16), b_full)


def _row_copy(src_hbm, dst_vmem, src_row, dst_row, sem):
    return pltpu.make_async_copy(src_hbm.at[pl.ds(src_row, 1)], dst_vmem.at[pl.ds(dst_row, 1)], sem)


def _moe_up_body(exp_ref, nact_ref, tok_ref, hp_hbm, w1_ref, w3_ref, o_ref, xbuf, xbf, sem, *, tile, gn):
    r = pl.program_id(0)
    j = pl.program_id(1)
    n_active = nact_ref[0]
    chunk = tile // gn
    half = xbuf.shape[2]

    def issue(row_tile, slot, row0, n_rows):
        base = row_tile * tile

        def f(k, c):
            row = row0 + k
            _row_copy(hp_hbm, xbuf.at[slot], tok_ref[base + row], row, sem.at[slot]).start()
            return c

        lax.fori_loop(0, n_rows, f, 0, unroll=8)

    @pl.when(jnp.logical_and(r == 0, j == 0))
    def _():
        issue(0, 0, 0, tile)

    @pl.when(r + 1 < n_active)
    def _():
        issue(r + 1, (r + 1) % 2, j * chunk, chunk)

    @pl.when(jnp.logical_and(r < n_active, j == 0))
    def _():
        slot = r % 2

        def drain(k, c):
            _row_copy(hp_hbm, xbuf.at[slot], 0, k, sem.at[slot]).wait()
            return c

        lax.fori_loop(0, tile, drain, 0, unroll=8)
        hi, lo = _unpack_bf16_pairs(xbuf[slot])
        xbf[:, :half] = hi
        xbf[:, half:] = lo

    @pl.when(r < n_active)
    def _():
        x = xbf[...]
        a = jnp.dot(x, w1_ref[...].astype(BF16), preferred_element_type=F32)
        b = jnp.dot(x, w3_ref[...].astype(BF16), preferred_element_type=F32)
        o_ref[...] = (jax.nn.silu(a) * b).astype(o_ref.dtype)

    @pl.when(r >= n_active)
    def _():
        o_ref[...] = jnp.zeros_like(o_ref)


def _moe_up(hp, slot_tok, tile_exp, n_active, w1, w3, lead, *, tile, tn, name):
    n_slots = slot_tok.shape[0]
    half = hp.shape[1]
    d, f = w1.shape[-2], w1.shape[-1]
    gn = f // tn
    assert tile % gn == 0

    def w_map(r, j, e, na, tok):
        re, je = _active_ij(r, j, na[0], gn)
        return (lead, e[re], 0, je)

    blk = 2 * _nbytes((d, tn), w1.dtype) + _nbytes((tile, tn), BF16)
    temp = (2 * _nbytes((tile, half), jnp.uint32) + _nbytes((tile, d), BF16)
            + 2 * _nbytes((d, tn), BF16) + 4 * tile * tn * 4)
    return pl.pallas_call(
        functools.partial(_moe_up_body, tile=tile, gn=gn),
        grid_spec=pltpu.PrefetchScalarGridSpec(
            num_scalar_prefetch=3, grid=(n_slots // tile, gn),
            in_specs=[pl.BlockSpec(memory_space=pl.ANY),
                      pl.BlockSpec((None, None, d, tn), w_map),
                      pl.BlockSpec((None, None, d, tn), w_map)],
            out_specs=pl.BlockSpec((tile, tn), lambda r, j, *pf: (r, j)),
            scratch_shapes=[pltpu.VMEM((2, tile, half), jnp.uint32), pltpu.VMEM((tile, d), BF16),
                            pltpu.SemaphoreType.DMA((2,))]),
        out_shape=jax.ShapeDtypeStruct((n_slots, f), BF16),
        compiler_params=pltpu.CompilerParams(
            dimension_semantics=("arbitrary", "arbitrary"), vmem_limit_bytes=_vmem_limit(blk, temp)),
        name=name,
    )(tile_exp, n_active, slot_tok, hp, w1, w3)


def _combine_body(slot_ref, x_ref, route_ref, g_ref, y_hbm, o_ref, buf, sem, *, rows, n_tiles, norm):
    i = pl.program_id(0)

    def issue(tile, slot):
        base = tile * rows

        def f(k, c):
            for j in range(TOP_K):
                _row_copy(y_hbm, buf.at[slot, j], slot_ref[(base + k) * TOP_K + j], k, sem.at[slot]).start()
            return c

        lax.fori_loop(0, rows, f, 0, unroll=4)

    @pl.when(i == 0)
    def _():
        issue(0, 0)

    @pl.when(i + 1 < n_tiles)
    def _():
        issue(i + 1, (i + 1) % 2)

    slot = i % 2

    def drain(k, c):
        for j in range(TOP_K):
            _row_copy(y_hbm, buf.at[slot, j], 0, k, sem.at[slot]).wait()
        return c

    lax.fori_loop(0, rows, drain, 0, unroll=4)
    route = route_ref[...]
    y = x_ref[...] + (buf[slot, 0] * route[:, 2:3] + buf[slot, 1] * route[:, 3:4])
    if norm:
        y = y * lax.rsqrt(jnp.mean(y * y, axis=-1, keepdims=True) + NORM_EPS) * g_ref[...]
    o_ref[...] = y


def _combine(x, route, slot_of_assign, yb, norm_gain, *, name):
    t, d = x.shape
    rows = _tile(t, 256, 8)
    n_tiles = t // rows
    norm = norm_gain is not None
    g = (norm_gain if norm else jnp.ones((d,), F32)).reshape(1, d).astype(F32)
    blk = 2 * _nbytes((rows, d), F32) + _nbytes((rows, V7X_LANES), F32)
    return pl.pallas_call(
        functools.partial(_combine_body, rows=rows, n_tiles=n_tiles, norm=norm),
        grid_spec=pltpu.PrefetchScalarGridSpec(
            num_scalar_prefetch=1, grid=(n_tiles,),
            in_specs=[pl.BlockSpec((rows, d), lambda i, s: (i, 0)),
                      pl.BlockSpec((rows, V7X_LANES), lambda i, s: (i, 0)),
                      pl.BlockSpec((1, d), lambda i, s: (0, 0)),
                      pl.BlockSpec(memory_space=pl.ANY)],
            out_specs=pl.BlockSpec((rows, d), lambda i, s: (i, 0)),
            scratch_shapes=[pltpu.VMEM((2, TOP_K, rows, d), F32), pltpu.SemaphoreType.DMA((2,))]),
        out_shape=jax.ShapeDtypeStruct((t, d), F32),
        compiler_params=pltpu.CompilerParams(
            dimension_semantics=("arbitrary",),
            vmem_limit_bytes=_vmem_limit(blk, (2 * TOP_K + 3) * _nbytes((rows, d), F32))),
        name=name,
    )(slot_of_assign, x, route, g, yb)


def _routing_tables(route, n_experts, tile):
    t = route.shape[0]
    a = t * TOP_K
    flat_e = route[:, :TOP_K].astype(jnp.int32).reshape(a)
    flat_tok = jnp.arange(a, dtype=jnp.int32) // TOP_K
    order = jnp.argsort(flat_e)
    se = flat_e[order]
    counts = jnp.zeros((n_experts,), jnp.int32).at[flat_e].add(1)
    padded = (counts + tile - 1) // tile * tile
    pad_end = jnp.cumsum(padded)
    pad_start = pad_end - padded
    grp_start = jnp.cumsum(counts) - counts
    dest = (pad_start[se] + (jnp.arange(a, dtype=jnp.int32) - grp_start[se])).astype(jnp.int32)
    n_tiles = -(-a // tile) + n_experts
    n_slots = n_tiles * tile
    slot_tok = jnp.zeros((n_slots,), jnp.int32).at[dest].set(flat_tok[order])
    slot_of_assign = jnp.zeros((a,), jnp.int32).at[order].set(dest)
    tile_start = jnp.arange(n_tiles, dtype=jnp.int32) * tile
    tile_exp = jnp.minimum(jnp.searchsorted(pad_end, tile_start, side="right"),
                           n_experts - 1).astype(jnp.int32)
    n_active = (pad_end[-1:] // tile).astype(jnp.int32)
    return slot_tok, slot_of_assign, tile_exp, n_active


def _rope_panels(positions):
    inv_freq = ROPE_THETA ** (-jnp.arange(0, QK_ROPE, 2, dtype=F32) / QK_ROPE)
    ang = positions.astype(F32).reshape(-1, 1) * inv_freq
    cos, sin = jnp.cos(ang), jnp.sin(ang)
    z = jnp.zeros_like(cos)
    pad = jnp.zeros((cos.shape[0], V7X_LANES - QK_ROPE), F32)
    cos_t = jnp.concatenate([cos, cos, pad], axis=1)
    sin_a = jnp.concatenate([z, sin, pad], axis=1)
    sin_b = jnp.concatenate([-sin, z, pad], axis=1)
    return cos_t, sin_a, sin_b


def kernel(x, positions, mix_norm, w_in, q_norm, kv_norm, w_uq, w_ukv, v_norm_g, v_norm_b, w_sp, b_sp, w_branch_a, w_branch_b, w_out, ffn_norm, dense_w1, dense_w3, dense_w2, w_router, moe_w1, moe_w3, moe_w2, final_norm):
    batch, seq, d = x.shape
    t = batch * seq
    depth = mix_norm.shape[0]
    ql, kvl = q_norm.shape[1], kv_norm.shape[1]
    heads = w_ukv.shape[2] // (QK_NOPE + V_HEAD)
    gw = v_norm_g.shape[1]
    o2 = ql + kvl
    o3 = o2 + QK_ROPE
    o4 = o3 + 2 * gw
    assert ql % kvl == 0
    tm = _tile(t, 1024, 8)
    tn_d = _tile(d, 512)
    w_in_t = jnp.swapaxes(w_in, 1, 2)

    rope = _rope_panels(positions)
    rope_ex = [(p, "m", 0) for p in rope]
    xf = x.reshape(t, d)
    out = None
    for layer in range(depth):
        w_q = jnp.pad(w_uq[layer].reshape(ql, heads, QK_NOPE + QK_ROPE),
                      ((0, 0), (0, 0), (0, HEAD_PAD - QK_NOPE - QK_ROPE))).reshape(ql, heads * HEAD_PAD).astype(BF16)
        w_kv = w_ukv[layer].astype(BF16)

        h = _rmsnorm(xf, mix_norm[layer], BF16, name=f"mix_norm{layer}")
        cqkv = _mm([h], [W(0, w_in_t, layer, nt=True)], _epi_id, [], o2, F32,
                   tm=tm, tn=_tile(o2, 512), name=f"in_latent{layer}")
        cq = _rmsnorm(cqkv, q_norm[layer], BF16, width=ql, col_block=0, name=f"q_norm{layer}")
        ckv = _rmsnorm(cqkv, kv_norm[layer], BF16, width=kvl, col_block=ql // kvl, name=f"kv_norm{layer}")
        k_rope = _mm([h], [W(0, w_in_t, layer, nt=True, row0=o2)], _epi_rope_all, rope_ex, V7X_LANES, BF16,
                     tm=tm, tn=V7X_LANES, name=f"in_krope{layer}")
        uv = _mm([h], [W(0, w_in_t, layer, nt=True, row0=o3)], _epi_gelu, [], 2 * gw, BF16,
                 tm=tm, tn=_tile(2 * gw, 512), name=f"in_uv{layer}")
        gates = _mm([h], [W(0, w_in_t, layer, nt=True, row0=o4)], _epi_sigmoid, [], 2 * d, BF16,
                    tm=tm, tn=_tile(2 * d, 512), name=f"in_gate{layer}")

        q = _mm([cq], [W(0, w_q)], _epi_q_heads, rope_ex, heads * HEAD_PAD, BF16,
                tm=tm, tn=_tile(heads * HEAD_PAD, 512, 2 * V7X_LANES), name=f"q_up{layer}")
        kv = _mm([ckv], [W(0, w_kv)], _epi_id, [], heads * (QK_NOPE + V_HEAD), BF16,
                 tm=tm, tn=_tile(heads * (QK_NOPE + V_HEAD), 512), name=f"kv_up{layer}")
        y_a = _attention(q, kv, k_rope, batch=batch, seq=seq, heads=heads, name=f"attention{layer}")
        y_b = _gmlp(uv, v_norm_g[layer], v_norm_b[layer], w_sp[layer], b_sp[layer], name=f"gmlp{layer}")

        merged = _mm([y_a, y_b], [W(0, w_branch_a, layer), W(1, w_branch_b, layer)],
                     _epi_gated_merge, [(gates, "mn", 0), (gates, "mn", d // tn_d)], d, BF16,
                     tm=tm, tn=tn_d, name=f"merge{layer}")
        xf = _mm([merged], [W(0, w_out, layer)], _epi_residual, [(xf, "mn", 0)], d, F32,
                 tm=tm, tn=tn_d, name=f"mix_out{layer}")

        i = layer // 2
        last = layer == depth - 1
        if layer % 2 == 0:
            ff = dense_w1.shape[2]
            h = _rmsnorm(xf, ffn_norm[layer], BF16, name=f"ffn_norm{layer}")
            act = _mm([h], [W(0, dense_w1, i), W(0, dense_w3, i)], _epi_swiglu, [], ff, BF16,
                      tm=tm, tn=_tile(ff, 256), name=f"dense_up{layer}")
            xf = _mm([act], [W(0, dense_w2[i].astype(BF16))], _epi_residual, [(xf, "mn", 0)], d, F32,
                     tm=_tile(t, 512, 8), tn=_tile(d, 256), name=f"dense_down{layer}")
            if last:
                out = _rmsnorm(xf, final_norm, F32, name="final_norm")
        else:
            n_experts = w_router.shape[2]
            fe = moe_w1.shape[3]
            hp, route = _norm_router(xf, ffn_norm[layer], w_router[i], name=f"ffn_norm_router{layer}")
            slot_tok, slot_of_assign, tile_exp, n_active = _routing_tables(route, n_experts, MOE_TILE)
            act = _moe_up(hp, slot_tok, tile_exp, n_active, moe_w1, moe_w3, i,
                          tile=MOE_TILE, tn=_tile(fe, 256), name=f"moe_up{layer}")
            yb = _mm([act], [W(0, moe_w2, i)], _epi_id, [], d, F32,
                     tm=MOE_TILE, tn=tn_d, group=(tile_exp, n_active), name=f"moe_down{layer}")
            xf = _combine(xf, route, slot_of_assign, yb, final_norm if last else None, name=f"moe_combine{layer}")
            if last:
                out = xf
    return out.reshape(batch, seq, d)
```

```python
import functools
import math
from typing import Any, NamedTuple

import jax
import jax.numpy as jnp
from jax import lax
from jax.experimental import pallas as pl
from jax.experimental.pallas import tpu as pltpu

F32 = jnp.float32
BF16 = jnp.bfloat16

V7X_LANES = 128
V7X_VMEM_BYTES = 64 * 1024 * 1024
V7X_VMEM_CAP = V7X_VMEM_BYTES - 6 * 1024 * 1024

QK_NOPE = 128
QK_ROPE = 64
V_HEAD = 128
HEAD_PAD = 256
GROUP_DIM = 128
CHUNK = 128
TOP_K = 2
MOE_TILE = 1024
NORM_EPS = 1e-6
ROPE_THETA = 10000.0
Q_PRESCALE = float(QK_NOPE + QK_ROPE) ** -0.5 * math.log2(math.e)


def _tile(n, pref, unit=V7X_LANES):
    t = (min(pref, n) // unit) * unit
    while t >= unit:
        if n % t == 0:
            return t
        t -= unit
    return n


def _vmem_limit(block_bytes, temp_bytes):
    return int(min(V7X_VMEM_CAP, 2 * block_bytes + temp_bytes + (4 << 20)))


def _nbytes(shape, dtype):
    n = 1
    for s in shape:
        n *= s
    return n * jnp.dtype(dtype).itemsize


class W(NamedTuple):
    xi: int
    arr: Any
    lead: Any = None
    off: int = 0
    nt: bool = False
    row0: int = 0


_NT_DIMS = (((1,), (1,)), ((), ()))


def _mm_compute(x_refs, w_refs, e_refs, ss_ref, gain_ref, o_refs, ws, epi, inv_k):
    accs = []
    for w, wr in zip(ws, w_refs):
        x = x_refs[w.xi][...]
        mat = wr[...].astype(BF16)
        if w.nt:
            accs.append(lax.dot_general(x, mat, _NT_DIMS, preferred_element_type=F32))
        else:
            accs.append(jnp.dot(x, mat, preferred_element_type=F32))
    if ss_ref is not None:
        r = lax.rsqrt(ss_ref[:, :1] * inv_k + NORM_EPS)
        accs = [a * r for a in accs]
    res = epi(accs, [e[...] for e in e_refs])
    o_refs[0][...] = res.astype(o_refs[0].dtype)
    if gain_ref is not None:
        xg_ref, sso_ref = o_refs[1], o_refs[2]
        xg_ref[...] = (res * gain_ref[...]).astype(xg_ref.dtype)
        part = jnp.broadcast_to(jnp.sum(res * res, axis=-1, keepdims=True), sso_ref.shape)
        j = pl.program_id(1)

        @pl.when(j == 0)
        def _():
            sso_ref[...] = part

        @pl.when(j > 0)
        def _():
            sso_ref[...] += part


def _mm_body(*refs, npf, nx, ws, ne, epi, prenorm, rowscale, norm_out, inv_k):
    pf = refs[:npf]
    refs = refs[npf:]
    x_refs, refs = refs[:nx], refs[nx:]
    w_refs, refs = refs[:len(ws)], refs[len(ws):]
    e_refs, refs = refs[:ne], refs[ne:]
    ss_ref = gain_ref = None
    if rowscale:
        ss_ref, refs = refs[0], refs[1:]
    if norm_out:
        gain_ref, refs = refs[0], refs[1:]
    n_out = 3 if norm_out else 1
    o_refs, refs = refs[:n_out], refs[n_out:]
    if prenorm:
        xn_ref = refs[0]

        @pl.when(pl.program_id(1) == 0)
        def _():
            _rmsnorm_body(x_refs[0], x_refs[1], xn_ref)

        x_refs = (xn_ref,)
    compute = functools.partial(_mm_compute, x_refs, w_refs, e_refs, ss_ref, gain_ref, o_refs, ws, epi, inv_k)
    if npf:
        active = pl.program_id(0) < pf[1][0]
        pl.when(active)(compute)

        @pl.when(jnp.logical_not(active))
        def _():
            o_refs[0][...] = jnp.zeros_like(o_refs[0])
    else:
        compute()


def _active_ij(i, j, n_active, gn):
    return jnp.minimum(i, n_active - 1), jnp.where(i < n_active, j, gn - 1)


def _mm(xs, ws, epi, extras, n, out_dtype, *, tm, tn, group=None, prenorm=None, rowscale=None, norm_out=None,
        name):
    m = xs[0].shape[0]
    gm, gn = m // tm, n // tn
    npf = 0 if group is None else 2

    def imap(f):
        if group is None:
            return f
        return lambda i, j, g, na: f(*_active_ij(i, j, na[0], gn), g)

    in_specs = []
    blk = 0
    temp = (len(ws) + 2) * tm * tn * 4
    scratch = []
    if prenorm is None:
        for x in xs:
            in_specs.append(pl.BlockSpec((tm, x.shape[1]), imap(lambda i, j, *pf: (i, 0))))
            blk += _nbytes((tm, x.shape[1]), x.dtype)
    else:
        gain, width, col_block = prenorm
        assert len(xs) == 1 and group is None
        in_specs.append(pl.BlockSpec((tm, width), lambda i, j: (i, col_block)))
        in_specs.append(pl.BlockSpec((1, width), lambda i, j: (0, 0)))
        xs = [xs[0], gain.reshape(1, width).astype(F32)]
        blk += _nbytes((tm, width), F32)
        scratch.append(pltpu.VMEM((tm, width), BF16))
        temp += _nbytes((tm, width), BF16) + 2 * _nbytes((tm, width), F32)
    for w in ws:
        lead = () if w.lead is None else (w.lead,)
        none = (None,) * len(lead)
        if w.nt:
            assert group is None and w.arr.ndim == 3
            k = w.arr.shape[-1]
            in_specs.append(pl.BlockSpec((pl.Squeezed(), pl.Element(tn), pl.Element(k)), imap(
                lambda i, j, *pf, lead=w.lead, row0=w.row0: (lead, pl.multiple_of(row0 + j * tn, 8), 0))))
        elif group is None:
            assert w.arr.ndim == 2 + len(lead)
            k = w.arr.shape[-2]
            in_specs.append(pl.BlockSpec(none + (k, tn), imap(
                lambda i, j, *pf, lead=lead, off=w.off: lead + (0, j + off))))
        else:
            assert w.arr.ndim == 3 + len(lead)
            k = w.arr.shape[-2]
            in_specs.append(pl.BlockSpec(none + (None, k, tn), imap(
                lambda i, j, g, lead=lead: lead + (g[i], 0, j))))
        blk += _nbytes((k, tn), w.arr.dtype)
        if w.arr.dtype != BF16:
            temp += _nbytes((k, tn), BF16)
    for arr, kind, off in extras:
        if kind == "mn":
            in_specs.append(pl.BlockSpec((tm, tn), imap(lambda i, j, *pf, off=off: (i, j + off))))
            blk += _nbytes((tm, tn), arr.dtype)
        else:
            in_specs.append(pl.BlockSpec((tm, arr.shape[1]), imap(lambda i, j, *pf: (i, 0))))
            blk += _nbytes((tm, arr.shape[1]), arr.dtype)
    tail = []
    if rowscale is not None:
        assert group is None
        in_specs.append(pl.BlockSpec((tm, V7X_LANES), lambda i, j: (i, 0)))
        blk += _nbytes((tm, V7X_LANES), F32)
        tail.append(rowscale)
    out_specs = [pl.BlockSpec((tm, tn), lambda i, j, *pf: (i, j))]
    out_shape = [jax.ShapeDtypeStruct((m, n), out_dtype)]
    blk += _nbytes((tm, tn), out_dtype)
    if norm_out is not None:
        assert group is None
        in_specs.append(pl.BlockSpec((1, tn), lambda i, j: (0, j)))
        tail.append(norm_out.reshape(1, n).astype(F32))
        out_specs += [pl.BlockSpec((tm, tn), lambda i, j: (i, j)), pl.BlockSpec((tm, V7X_LANES), lambda i, j: (i, 0))]
        out_shape += [jax.ShapeDtypeStruct((m, n), BF16), jax.ShapeDtypeStruct((m, V7X_LANES), F32)]
        blk += _nbytes((tm, tn), BF16) + _nbytes((tm, V7X_LANES), F32)
    inv_k = 1.0 / xs[0].shape[1]
    body = functools.partial(_mm_body, npf=npf, nx=len(xs), ws=tuple(w._replace(arr=None) for w in ws),
                             ne=len(extras), epi=epi, prenorm=prenorm is not None,
                             rowscale=rowscale is not None, norm_out=norm_out is not None, inv_k=inv_k)
    call = pl.pallas_call(
        body,
        grid_spec=pltpu.PrefetchScalarGridSpec(
            num_scalar_prefetch=npf, grid=(gm, gn), in_specs=in_specs, out_specs=out_specs,
            scratch_shapes=scratch),
        out_shape=out_shape,
        compiler_params=pltpu.CompilerParams(
            dimension_semantics=("arbitrary", "arbitrary"), vmem_limit_bytes=_vmem_limit(blk, temp)),
        name=name,
    )
    args = [] if group is None else list(group)
    args += list(xs) + [w.arr for w in ws] + [a for a, _, _ in extras] + tail
    res = call(*args)
    return res if norm_out is not None else res[0]


def _epi_id(accs, ex):
    return accs[0]


def _epi_gelu(accs, ex):
    a = accs[0]
    return 0.5 * a * (1.0 + lax.erf(a * (2.0 ** -0.5)))


def _epi_sigmoid(accs, ex):
    return jax.nn.sigmoid(accs[0])


def _epi_residual(accs, ex):
    return ex[0] + accs[0]


def _epi_swiglu(accs, ex):
    return jax.nn.silu(accs[0]) * accs[1]


def _epi_gated_merge(accs, ex):
    return ex[0].astype(F32) * accs[0] + ex[1].astype(F32) * accs[1]


def _rope_lanes(t, cos_t, sin_a, sin_b):
    half = QK_ROPE // 2
    return (t * cos_t + pltpu.roll(t, half, 1) * sin_a
            + pltpu.roll(t, V7X_LANES - half, 1) * sin_b)


def _epi_rope_all(accs, ex):
    return _rope_lanes(accs[0], *ex)


def _epi_q_heads(accs, ex):
    a = accs[0]
    outs = []
    for g in range(a.shape[1] // V7X_LANES):
        t = a[:, g * V7X_LANES:(g + 1) * V7X_LANES]
        outs.append((_rope_lanes(t, *ex) if g % 2 else t) * Q_PRESCALE)
    return jnp.concatenate(outs, axis=1)


def _rmsnorm_body(x_ref, g_ref, o_ref):
    x = x_ref[...].astype(F32)
    y = x * lax.rsqrt(jnp.mean(x * x, axis=-1, keepdims=True) + NORM_EPS)
    o_ref[...] = (y * g_ref[...]).astype(o_ref.dtype)


def _rmsnorm(x, g, out_dtype, *, width=None, col_block=0, name):
    m = x.shape[0]
    width = x.shape[1] if width is None else width
    tm = _tile(m, 256, 8)
    blk = _nbytes((tm, width), x.dtype) + _nbytes((tm, width), out_dtype)
    return pl.pallas_call(
        _rmsnorm_body,
        grid=(m // tm,),
        in_specs=[pl.BlockSpec((tm, width), lambda i: (i, col_block)),
                  pl.BlockSpec((1, width), lambda i: (0, 0))],
        out_specs=pl.BlockSpec((tm, width), lambda i: (i, 0)),
        out_shape=jax.ShapeDtypeStruct((m, width), out_dtype),
        compiler_params=pltpu.CompilerParams(
            dimension_semantics=("arbitrary",), vmem_limit_bytes=_vmem_limit(blk, 3 * tm * width * 4)),
        name=name,
    )(x, g.reshape(1, width).astype(F32))


def _pack_bf16_pairs(h):
    half = h.shape[1] // 2
    bits = lax.bitcast_convert_type(h.astype(BF16).astype(F32), jnp.uint32)
    return bits[:, :half] | (bits[:, half:] >> 16)


def _unpack_bf16_pairs(p):
    hi = lax.bitcast_convert_type(p & jnp.uint32(0xFFFF0000), F32).astype(BF16)
    lo = lax.bitcast_convert_type(p << 16, F32).astype(BF16)
    return hi, lo


def _norm_router_body(x_ref, g_ref, wr_ref, h_ref, route_ref, *, n_experts):
    x = x_ref[...]
    h = x * lax.rsqrt(jnp.mean(x * x, axis=-1, keepdims=True) + NORM_EPS) * g_ref[...]
    h_ref[...] = _pack_bf16_pairs(h)
    logits = jnp.dot(h, wr_ref[...], preferred_element_type=F32, precision=lax.Precision.HIGHEST)
    lane = lax.broadcasted_iota(jnp.int32, logits.shape, 1).astype(F32)
    neg = jnp.float32(-jnp.inf)
    far = jnp.float32(V7X_LANES)
    l1 = jnp.where(lane < n_experts, logits, neg)
    m1 = jnp.max(l1, axis=-1, keepdims=True)
    i1 = jnp.min(jnp.where(l1 == m1, lane, far), axis=-1, keepdims=True)
    l2 = jnp.where(lane == i1, neg, l1)
    m2 = jnp.max(l2, axis=-1, keepdims=True)
    i2 = jnp.min(jnp.where(l2 == m2, lane, far), axis=-1, keepdims=True)
    e = jnp.exp(m2 - m1)
    g1 = 1.0 / (1.0 + e)
    g2 = e / (1.0 + e)
    route = jnp.where(lane == 0, i1, jnp.where(lane == 1, i2, jnp.where(lane == 2, g1, jnp.where(lane == 3, g2, 0.0))))
    route_ref[...] = route


def _norm_router(x, g, w_router, *, name):
    m, d = x.shape
    n_experts = w_router.shape[1]
    wr = jnp.pad(w_router.astype(F32), ((0, 0), (0, V7X_LANES - n_experts)))
    tm = _tile(m, 256, 8)
    blk = 2 * _nbytes((tm, d), F32) + _nbytes((d, V7X_LANES), F32) + _nbytes((tm, V7X_LANES), F32)
    return pl.pallas_call(
        functools.partial(_norm_router_body, n_experts=n_experts),
        grid=(m // tm,),
        in_specs=[pl.BlockSpec((tm, d), lambda i: (i, 0)),
                  pl.BlockSpec((1, d), lambda i: (0, 0)),
                  pl.BlockSpec((d, V7X_LANES), lambda i: (0, 0))],
        out_specs=[pl.BlockSpec((tm, d // 2), lambda i: (i, 0)),
                   pl.BlockSpec((tm, V7X_LANES), lambda i: (i, 0))],
        out_shape=[jax.ShapeDtypeStruct((m, d // 2), jnp.uint32), jax.ShapeDtypeStruct((m, V7X_LANES), F32)],
        compiler_params=pltpu.CompilerParams(
            dimension_semantics=("arbitrary",), vmem_limit_bytes=_vmem_limit(blk, 4 * tm * d * 4)),
        name=name,
    )(x, g.reshape(1, d).astype(F32), wr)


def _attn_body(q_ref, kn_ref, kr_ref, v_ref, o_ref, kcat_ref, *, sub):
    @pl.when(pl.program_id(2) == 0)
    def _():
        kcat_ref[:, :QK_NOPE] = kn_ref[...]
        kcat_ref[:, QK_NOPE:] = kr_ref[...]

    for r0 in range(0, q_ref.shape[0], sub):
        rows = slice(r0, r0 + sub)
        s = lax.dot_general(q_ref[rows, :], kcat_ref[...], (((1,), (1,)), ((), ())),
                            preferred_element_type=F32)
        p = jnp.exp2(s - jnp.max(s, axis=-1, keepdims=True))
        l = jnp.sum(p, axis=-1, keepdims=True)
        o = jnp.dot(p.astype(BF16), v_ref[...], preferred_element_type=F32)
        o_ref[rows, :] = (o / l).astype(o_ref.dtype)


def _attention(q, kv, k_rope, *, batch, seq, heads, name):
    t = batch * seq
    sub = _tile(seq, 256, 8)
    tq = _tile(seq, 8 * sub, sub)
    nq = seq // tq
    blk = (_nbytes((tq, HEAD_PAD), BF16) + 3 * _nbytes((seq, V7X_LANES), BF16) + _nbytes((tq, V_HEAD), BF16))
    return pl.pallas_call(
        functools.partial(_attn_body, sub=sub),
        grid=(batch, heads, nq),
        in_specs=[pl.BlockSpec((tq, HEAD_PAD), lambda b, h, i: (b * nq + i, h)),
                  pl.BlockSpec((seq, QK_NOPE), lambda b, h, i: (b, 2 * h)),
                  pl.BlockSpec((seq, V7X_LANES), lambda b, h, i: (b, 0)),
                  pl.BlockSpec((seq, V_HEAD), lambda b, h, i: (b, 2 * h + 1))],
        out_specs=pl.BlockSpec((tq, V_HEAD), lambda b, h, i: (b * nq + i, h)),
        out_shape=jax.ShapeDtypeStruct((t, heads * V_HEAD), BF16),
        scratch_shapes=[pltpu.VMEM((seq, HEAD_PAD), BF16)],
        compiler_params=pltpu.CompilerParams(
            dimension_semantics=("arbitrary", "arbitrary", "arbitrary"),
            vmem_limit_bytes=_vmem_limit(blk, _nbytes((seq, HEAD_PAD), BF16) + 4 * tq * seq * 4)),
        name=name,
    )(q, kv, k_rope, kv)


def _gmlp_body(u_ref, v_ref, g_ref, b_ref, w_ref, bs_ref, o_ref, *, n_chunks, n_groups):
    v = v_ref[...].astype(F32)
    mu = jnp.mean(v, axis=-1, keepdims=True)
    vc = v - mu
    vn = vc * lax.rsqrt(jnp.mean(vc * vc, axis=-1, keepdims=True) + NORM_EPS)
    vn = (vn * g_ref[...] + b_ref[...]).astype(BF16)
    for c in range(n_chunks):
        rows = slice(c * CHUNK, (c + 1) * CHUNK)
        for g in range(n_groups):
            cols = slice(g * GROUP_DIM, (g + 1) * GROUP_DIM)
            s = jnp.dot(w_ref[g], vn[rows, cols], preferred_element_type=F32) + bs_ref[g]
            o_ref[rows, cols] = (u_ref[rows, cols].astype(F32) * s).astype(o_ref.dtype)


def _gmlp(uv, v_norm_g, v_norm_b, w_sp, b_sp, *, name):
    t = uv.shape[0]
    gw = uv.shape[1] // 2
    n_groups = w_sp.shape[0]
    rows = _tile(t, 2 * CHUNK, CHUNK)
    b_full = jnp.broadcast_to(b_sp.astype(F32)[:, :, None], (n_groups, CHUNK, GROUP_DIM))
    blk = 3 * _nbytes((rows, gw), BF16) + _nbytes(w_sp.shape, BF16) + _nbytes(b_full.shape, F32)
    return pl.pallas_call(
        functools.partial(_gmlp_body, n_chunks=rows // CHUNK, n_groups=n_groups),
        grid=(t // rows,),
        in_specs=[pl.BlockSpec((rows, gw), lambda i: (i, 0)),
                  pl.BlockSpec((rows, gw), lambda i: (i, 1)),
                  pl.BlockSpec((1, gw), lambda i: (0, 0)),
                  pl.BlockSpec((1, gw), lambda i: (0, 0)),
                  pl.BlockSpec((n_groups, CHUNK, CHUNK), lambda i: (0, 0, 0)),
                  pl.BlockSpec((n_groups, CHUNK, GROUP_DIM), lambda i: (0, 0, 0))],
        out_specs=pl.BlockSpec((rows, gw), lambda i: (i, 0)),
        out_shape=jax.ShapeDtypeStruct((t, gw), BF16),
        compiler_params=pltpu.CompilerParams(
            dimension_semantics=("arbitrary",), vmem_limit_bytes=_vmem_limit(blk, 4 * rows * gw * 4)),
        name=name,
    )(uv, uv, v_norm_g.reshape(1, gw).astype(F32), v_norm_b.reshape(1, gw).astype(F32),
      w_sp.astype(BF16), b_full)


def _row_copy(src_hbm, dst_vmem, src_row, dst_row, sem):
    return pltpu.make_async_copy(src_hbm.at[pl.ds(src_row, 1)], dst_vmem.at[pl.ds(dst_row, 1)], sem)


def _moe_up_body(exp_ref, nact_ref, tok_ref, hp_hbm, w1_ref, w3_ref, o_ref, xbuf, xbf, sem, *, tile, gn):
    r = pl.program_id(0)
    j = pl.program_id(1)
    n_active = nact_ref[0]
    chunk = tile // gn
    half = xbuf.shape[2]

    def issue(row_tile, slot, row0, n_rows):
        base = row_tile * tile

        def f(k, c):
            row = row0 + k
            _row_copy(hp_hbm, xbuf.at[slot], tok_ref[base + row], row, sem.at[slot]).start()
            return c

        lax.fori_loop(0, n_rows, f, 0, unroll=8)

    @pl.when(jnp.logical_and(r == 0, j == 0))
    def _():
        issue(0, 0, 0, tile)

    @pl.when(jnp.logical_and(r < n_active, j == 0))
    def _():
        slot = r % 2

        def drain(k, c):
            _row_copy(hp_hbm, xbuf.at[slot], 0, k, sem.at[slot]).wait()
            return c

        lax.fori_loop(0, tile, drain, 0, unroll=8)
        hi, lo = _unpack_bf16_pairs(xbuf[slot])
        xbf[:, :half] = hi
        xbf[:, half:] = lo

    def compute():
        x = xbf[...]
        a = jnp.dot(x, w1_ref[...].astype(BF16), preferred_element_type=F32)
        b = jnp.dot(x, w3_ref[...].astype(BF16), preferred_element_type=F32)
        o_ref[...] = (jax.nn.silu(a) * b).astype(o_ref.dtype)

    has_next = r + 1 < n_active

    @pl.when(jnp.logical_and(r < n_active, has_next))
    def _():
        compute()
        base = (r + 1) * tile + j * chunk
        slot = (r + 1) % 2
        toks = [tok_ref[base + k] for k in range(chunk)]
        for k in range(chunk):
            _row_copy(hp_hbm, xbuf.at[slot], toks[k], j * chunk + k, sem.at[slot]).start()

    @pl.when(jnp.logical_and(r < n_active, jnp.logical_not(has_next)))
    def _():
        compute()

    @pl.when(r >= n_active)
    def _():
        o_ref[...] = jnp.zeros_like(o_ref)


def _moe_up(hp, slot_tok, tile_exp, n_active, w1, w3, lead, *, tile, tn, name):
    n_slots = slot_tok.shape[0]
    half = hp.shape[1]
    d, f = w1.shape[-2], w1.shape[-1]
    gn = f // tn
    assert tile % gn == 0

    def w_map(r, j, e, na, tok):
        re, je = _active_ij(r, j, na[0], gn)
        return (lead, e[re], 0, je)

    blk = 2 * _nbytes((d, tn), w1.dtype) + _nbytes((tile, tn), BF16)
    temp = (2 * _nbytes((tile, half), jnp.uint32) + _nbytes((tile, d), BF16)
            + 2 * _nbytes((d, tn), BF16) + 4 * tile * tn * 4)
    return pl.pallas_call(
        functools.partial(_moe_up_body, tile=tile, gn=gn),
        grid_spec=pltpu.PrefetchScalarGridSpec(
            num_scalar_prefetch=3, grid=(n_slots // tile, gn),
            in_specs=[pl.BlockSpec(memory_space=pl.ANY),
                      pl.BlockSpec((None, None, d, tn), w_map),
                      pl.BlockSpec((None, None, d, tn), w_map)],
            out_specs=pl.BlockSpec((tile, tn), lambda r, j, *pf: (r, j)),
            scratch_shapes=[pltpu.VMEM((2, tile, half), jnp.uint32), pltpu.VMEM((tile, d), BF16),
                            pltpu.SemaphoreType.DMA((2,))]),
        out_shape=jax.ShapeDtypeStruct((n_slots, f), BF16),
        compiler_params=pltpu.CompilerParams(
            dimension_semantics=("arbitrary", "arbitrary"), vmem_limit_bytes=_vmem_limit(blk, temp)),
        name=name,
    )(tile_exp, n_active, slot_tok, hp, w1, w3)


def _combine_body(slot_ref, x_ref, route_ref, g_ref, y_hbm, o_ref, buf, sem, *, rows, n_tiles, norm):
    i = pl.program_id(0)

    def issue(tile, slot):
        base = tile * rows

        def f(k, c):
            for j in range(TOP_K):
                _row_copy(y_hbm, buf.at[slot, j], slot_ref[(base + k) * TOP_K + j], k, sem.at[slot]).start()
            return c

        lax.fori_loop(0, rows, f, 0, unroll=4)

    @pl.when(i == 0)
    def _():
        issue(0, 0)

    @pl.when(i + 1 < n_tiles)
    def _():
        issue(i + 1, (i + 1) % 2)

    slot = i % 2

    def drain(k, c):
        for j in range(TOP_K):
            _row_copy(y_hbm, buf.at[slot, j], 0, k, sem.at[slot]).wait()
        return c

    lax.fori_loop(0, rows, drain, 0, unroll=4)
    route = route_ref[...]
    y = x_ref[...] + (buf[slot, 0] * route[:, 2:3] + buf[slot, 1] * route[:, 3:4])
    if norm:
        y = y * lax.rsqrt(jnp.mean(y * y, axis=-1, keepdims=True) + NORM_EPS) * g_ref[...]
    o_ref[...] = y


def _combine(x, route, slot_of_assign, yb, norm_gain, *, name):
    t, d = x.shape
    rows = _tile(t, 256, 8)
    n_tiles = t // rows
    norm = norm_gain is not None
    g = (norm_gain if norm else jnp.ones((d,), F32)).reshape(1, d).astype(F32)
    blk = 2 * _nbytes((rows, d), F32) + _nbytes((rows, V7X_LANES), F32)
    return pl.pallas_call(
        functools.partial(_combine_body, rows=rows, n_tiles=n_tiles, norm=norm),
        grid_spec=pltpu.PrefetchScalarGridSpec(
            num_scalar_prefetch=1, grid=(n_tiles,),
            in_specs=[pl.BlockSpec((rows, d), lambda i, s: (i, 0)),
                      pl.BlockSpec((rows, V7X_LANES), lambda i, s: (i, 0)),
                      pl.BlockSpec((1, d), lambda i, s: (0, 0)),
                      pl.BlockSpec(memory_space=pl.ANY)],
            out_specs=pl.BlockSpec((rows, d), lambda i, s: (i, 0)),
            scratch_shapes=[pltpu.VMEM((2, TOP_K, rows, d), F32), pltpu.SemaphoreType.DMA((2,))]),
        out_shape=jax.ShapeDtypeStruct((t, d), F32),
        compiler_params=pltpu.CompilerParams(
            dimension_semantics=("arbitrary",),
            vmem_limit_bytes=_vmem_limit(blk, (2 * TOP_K + 3) * _nbytes((rows, d), F32))),
        name=name,
    )(slot_of_assign, x, route, g, yb)


def _routing_tables(route, n_experts, tile):
    t = route.shape[0]
    a = t * TOP_K
    flat_e = route[:, :TOP_K].astype(jnp.int32).reshape(a)
    order = jnp.argsort(flat_e).astype(jnp.int32)
    rank = jnp.argsort(order).astype(jnp.int32)
    counts = jnp.sum(flat_e[:, None] == jnp.arange(n_experts, dtype=jnp.int32)[None, :], axis=0, dtype=jnp.int32)
    padded = (counts + tile - 1) // tile * tile
    pad_end = jnp.cumsum(padded)
    pad_start = pad_end - padded
    shift = pad_start - (jnp.cumsum(counts) - counts)
    slot_of_assign = rank + shift[flat_e]
    n_tiles = -(-a // tile) + n_experts
    tile_start = jnp.arange(n_tiles, dtype=jnp.int32) * tile
    tile_exp = jnp.minimum(jnp.searchsorted(pad_end, tile_start, side="right"),
                           n_experts - 1).astype(jnp.int32)
    slot = jnp.arange(n_tiles * tile, dtype=jnp.int32)
    slot_exp = jnp.repeat(tile_exp, tile)
    real = slot - pad_start[slot_exp] < counts[slot_exp]
    slot_tok = jnp.where(real, order[jnp.clip(slot - shift[slot_exp], 0, a - 1)] // TOP_K, 0)
    n_active = (pad_end[-1:] // tile).astype(jnp.int32)
    return slot_tok.astype(jnp.int32), slot_of_assign.astype(jnp.int32), tile_exp, n_active


def _rope_panels(positions):
    inv_freq = ROPE_THETA ** (-jnp.arange(0, QK_ROPE, 2, dtype=F32) / QK_ROPE)
    ang = positions.astype(F32).reshape(-1, 1) * inv_freq
    cos, sin = jnp.cos(ang), jnp.sin(ang)
    z = jnp.zeros_like(cos)
    pad = jnp.zeros((cos.shape[0], V7X_LANES - QK_ROPE), F32)
    cos_t = jnp.concatenate([cos, cos, pad], axis=1)
    sin_a = jnp.concatenate([z, sin, pad], axis=1)
    sin_b = jnp.concatenate([-sin, z, pad], axis=1)
    return cos_t, sin_a, sin_b


def kernel(x, positions, mix_norm, w_in, q_norm, kv_norm, w_uq, w_ukv, v_norm_g, v_norm_b, w_sp, b_sp, w_branch_a, w_branch_b, w_out, ffn_norm, dense_w1, dense_w3, dense_w2, w_router, moe_w1, moe_w3, moe_w2, final_norm):
    batch, seq, d = x.shape
    t = batch * seq
    depth = mix_norm.shape[0]
    ql, kvl = q_norm.shape[1], kv_norm.shape[1]
    heads = w_ukv.shape[2] // (QK_NOPE + V_HEAD)
    gw = v_norm_g.shape[1]
    o2 = ql + kvl
    o3 = o2 + QK_ROPE
    o4 = o3 + 2 * gw
    assert ql % kvl == 0
    tm = _tile(t, 1024, 8)
    tn_d = _tile(d, 512)
    w_in_t = jnp.swapaxes(w_in, 1, 2)

    rope = _rope_panels(positions)
    rope_ex = [(p, "m", 0) for p in rope]
    xf = x.reshape(t, d)
    out = h = h_ss = None
    for layer in range(depth):
        w_q = jnp.pad(w_uq[layer].reshape(ql, heads, QK_NOPE + QK_ROPE),
                      ((0, 0), (0, 0), (0, HEAD_PAD - QK_NOPE - QK_ROPE))).reshape(ql, heads * HEAD_PAD).astype(BF16)
        w_kv = w_ukv[layer].astype(BF16)

        if h is None:
            h, h_ss = _rmsnorm(xf, mix_norm[layer], BF16, name=f"mix_norm{layer}"), None
        cqkv = _mm([h], [W(0, w_in_t, layer, nt=True)], _epi_id, [], o2, F32,
                   tm=tm, tn=_tile(o2, 512), rowscale=h_ss, name=f"in_latent{layer}")
        k_rope = _mm([h], [W(0, w_in_t, layer, nt=True, row0=o2)], _epi_rope_all, rope_ex, V7X_LANES, BF16,
                     tm=tm, tn=V7X_LANES, rowscale=h_ss, name=f"in_krope{layer}")
        uv = _mm([h], [W(0, w_in_t, layer, nt=True, row0=o3)], _epi_gelu, [], 2 * gw, BF16,
                 tm=tm, tn=_tile(2 * gw, 512), rowscale=h_ss, name=f"in_uv{layer}")
        gates = _mm([h], [W(0, w_in_t, layer, nt=True, row0=o4)], _epi_sigmoid, [], 2 * d, BF16,
                    tm=tm, tn=_tile(2 * d, 512), rowscale=h_ss, name=f"in_gate{layer}")
        h = None

        q = _mm([cqkv], [W(0, w_q)], _epi_q_heads, rope_ex, heads * HEAD_PAD, BF16,
                tm=tm, tn=_tile(heads * HEAD_PAD, 512, 2 * V7X_LANES),
                prenorm=(q_norm[layer], ql, 0), name=f"q_up{layer}")
        kv = _mm([cqkv], [W(0, w_kv)], _epi_id, [], heads * (QK_NOPE + V_HEAD), BF16,
                 tm=tm, tn=_tile(heads * (QK_NOPE + V_HEAD), 512),
                 prenorm=(kv_norm[layer], kvl, ql // kvl), name=f"kv_up{layer}")
        y_a = _attention(q, kv, k_rope, batch=batch, seq=seq, heads=heads, name=f"attention{layer}")
        y_b = _gmlp(uv, v_norm_g[layer], v_norm_b[layer], w_sp[layer], b_sp[layer], name=f"gmlp{layer}")

        merged = _mm([y_a, y_b], [W(0, w_branch_a, layer), W(1, w_branch_b, layer)],
                     _epi_gated_merge, [(gates, "mn", 0), (gates, "mn", d // tn_d)], d, BF16,
                     tm=tm, tn=tn_d, name=f"merge{layer}")
        i = layer // 2
        last = layer == depth - 1
        dense = layer % 2 == 0
        xf = _mm([merged], [W(0, w_out, layer)], _epi_residual, [(xf, "mn", 0)], d, F32,
                 tm=tm, tn=tn_d, norm_out=ffn_norm[layer] if dense else None, name=f"mix_out{layer}")
        if dense:
            ff = dense_w1.shape[2]
            xf, hg, hg_ss = xf
            act = _mm([hg], [W(0, dense_w1, i), W(0, dense_w3, i)], _epi_swiglu, [], ff, BF16,
                      tm=tm, tn=_tile(ff, 256), rowscale=hg_ss, name=f"dense_up{layer}")
            xf = _mm([act], [W(0, dense_w2[i].astype(BF16))], _epi_residual, [(xf, "mn", 0)], d, F32,
                     tm=_tile(t, 512, 8), tn=_tile(d, 512),
                     norm_out=None if last else mix_norm[layer + 1], name=f"dense_down{layer}")
            if last:
                out = _rmsnorm(xf, final_norm, F32, name="final_norm")
            else:
                xf, h, h_ss = xf
        else:
            n_experts = w_router.shape[2]
            fe = moe_w1.shape[3]
            hp, route = _norm_router(xf, ffn_norm[layer], w_router[i], name=f"ffn_norm_router{layer}")
            slot_tok, slot_of_assign, tile_exp, n_active = _routing_tables(route, n_experts, MOE_TILE)
            act = _moe_up(hp, slot_tok, tile_exp, n_active, moe_w1, moe_w3, i,
                          tile=MOE_TILE, tn=_tile(fe, 256), name=f"moe_up{layer}")
            yb = _mm([act], [W(0, moe_w2, i)], _epi_id, [], d, F32,
                     tm=MOE_TILE, tn=tn_d, group=(tile_exp, n_active), name=f"moe_down{layer}")
            xf = _combine(xf, route, slot_of_assign, yb, final_norm if last else None, name=f"moe_combine{layer}")
            if last:
                out = xf
    return out.reshape(batch, seq, d)
```

```python
import functools
import math
from typing import Any, NamedTuple

import jax
import jax.numpy as jnp
from jax import lax
from jax.experimental import pallas as pl
from jax.experimental.pallas import tpu as pltpu

F32 = jnp.float32
BF16 = jnp.bfloat16

V7X_LANES = 128
V7X_VMEM_BYTES = 64 * 1024 * 1024
V7X_VMEM_CAP = V7X_VMEM_BYTES - 6 * 1024 * 1024

QK_NOPE = 128
QK_ROPE = 64
V_HEAD = 128
HEAD_PAD = 256
GROUP_DIM = 128
CHUNK = 128
TOP_K = 2
MOE_TILE = 1024
NORM_EPS = 1e-6
ROPE_THETA = 10000.0
Q_PRESCALE = float(QK_NOPE + QK_ROPE) ** -0.5 * math.log2(math.e)


def _tile(n, pref, unit=V7X_LANES):
    t = (min(pref, n) // unit) * unit
    while t >= unit:
        if n % t == 0:
            return t
        t -= unit
    return n


def _vmem_limit(block_bytes, temp_bytes):
    return int(min(V7X_VMEM_CAP, 2 * block_bytes + temp_bytes + (4 << 20)))


def _nbytes(shape, dtype):
    n = 1
    for s in shape:
        n *= s
    return n * jnp.dtype(dtype).itemsize


class W(NamedTuple):
    xi: int
    arr: Any
    lead: Any = None
    off: int = 0
    nt: bool = False
    row0: int = 0


_NT_DIMS = (((1,), (1,)), ((), ()))


def _mm_compute(x_refs, w_refs, e_refs, ss_ref, gain_ref, o_refs, ws, epi, inv_k, nrows=None):
    if nrows is not None:
        assert not e_refs and ss_ref is None and gain_ref is None
        o_ref = o_refs[0]
        x_refs = [x.at[pl.ds(0, nrows)] for x in x_refs]
        o_refs = [o_ref.at[pl.ds(0, nrows)]]
        o_ref[nrows:, :] = jnp.zeros((o_ref.shape[0] - nrows, o_ref.shape[1]), o_ref.dtype)
    accs = []
    for w, wr in zip(ws, w_refs):
        x = x_refs[w.xi][...]
        mat = wr[...].astype(BF16)
        if w.nt:
            accs.append(lax.dot_general(x, mat, _NT_DIMS, preferred_element_type=F32))
        else:
            accs.append(jnp.dot(x, mat, preferred_element_type=F32))
    if ss_ref is not None:
        r = lax.rsqrt(ss_ref[:, :1] * inv_k + NORM_EPS)
        accs = [a * r for a in accs]
    res = epi(accs, [e[...] for e in e_refs])
    o_refs[0][...] = res.astype(o_refs[0].dtype)
    if gain_ref is not None:
        xg_ref, sso_ref = o_refs[1], o_refs[2]
        xg_ref[...] = (res * gain_ref[...]).astype(xg_ref.dtype)
        part = jnp.broadcast_to(jnp.sum(res * res, axis=-1, keepdims=True), sso_ref.shape)
        j = pl.program_id(1)

        @pl.when(j == 0)
        def _():
            sso_ref[...] = part

        @pl.when(j > 0)
        def _():
            sso_ref[...] += part


def _mm_body(*refs, npf, nx, ws, ne, epi, prenorm, rowscale, norm_out, inv_k):
    pf = refs[:npf]
    refs = refs[npf:]
    x_refs, refs = refs[:nx], refs[nx:]
    w_refs, refs = refs[:len(ws)], refs[len(ws):]
    e_refs, refs = refs[:ne], refs[ne:]
    ss_ref = gain_ref = None
    if rowscale:
        ss_ref, refs = refs[0], refs[1:]
    if norm_out:
        gain_ref, refs = refs[0], refs[1:]
    n_out = 3 if norm_out else 1
    o_refs, refs = refs[:n_out], refs[n_out:]
    if prenorm:
        xn_ref = refs[0]

        @pl.when(pl.program_id(1) == 0)
        def _():
            _rmsnorm_body(x_refs[0], x_refs[1], xn_ref)

        x_refs = (xn_ref,)
    compute = functools.partial(_mm_compute, x_refs, w_refs, e_refs, ss_ref, gain_ref, o_refs, ws, epi, inv_k)
    if npf:
        rows = pf[2][pl.program_id(0)]
        half = o_refs[0].shape[0] // 2
        pl.when(rows > half)(compute)
        pl.when(jnp.logical_and(rows > 0, rows <= half))(functools.partial(compute, nrows=half))

        @pl.when(rows == 0)
        def _():
            o_refs[0][...] = jnp.zeros_like(o_refs[0])
    else:
        compute()


def _active_ij(i, j, n_active, gn):
    return jnp.minimum(i, n_active - 1), jnp.where(i < n_active, j, gn - 1)


def _mm(xs, ws, epi, extras, n, out_dtype, *, tm, tn, group=None, prenorm=None, rowscale=None, norm_out=None,
        name):
    m = xs[0].shape[0]
    gm, gn = m // tm, n // tn
    npf = 0 if group is None else 3

    def imap(f):
        if group is None:
            return f
        return lambda i, j, g, na, rows: f(*_active_ij(i, j, na[0], gn), g)

    in_specs = []
    blk = 0
    temp = (len(ws) + 2) * tm * tn * 4
    scratch = []
    if prenorm is None:
        for x in xs:
            in_specs.append(pl.BlockSpec((tm, x.shape[1]), imap(lambda i, j, *pf: (i, 0))))
            blk += _nbytes((tm, x.shape[1]), x.dtype)
    else:
        gain, width, col_block = prenorm
        assert len(xs) == 1 and group is None
        in_specs.append(pl.BlockSpec((tm, width), lambda i, j: (i, col_block)))
        in_specs.append(pl.BlockSpec((1, width), lambda i, j: (0, 0)))
        xs = [xs[0], gain.reshape(1, width).astype(F32)]
        blk += _nbytes((tm, width), F32)
        scratch.append(pltpu.VMEM((tm, width), BF16))
        temp += _nbytes((tm, width), BF16) + 2 * _nbytes((tm, width), F32)
    for w in ws:
        lead = () if w.lead is None else (w.lead,)
        none = (None,) * len(lead)
        if w.nt:
            assert group is None and w.arr.ndim == 3
            k = w.arr.shape[-1]
            in_specs.append(pl.BlockSpec((pl.Squeezed(), pl.Element(tn), pl.Element(k)), imap(
                lambda i, j, *pf, lead=w.lead, row0=w.row0: (lead, pl.multiple_of(row0 + j * tn, 8), 0))))
        elif group is None:
            assert w.arr.ndim == 2 + len(lead)
            k = w.arr.shape[-2]
            in_specs.append(pl.BlockSpec(none + (k, tn), imap(
                lambda i, j, *pf, lead=lead, off=w.off: lead + (0, j + off))))
        else:
            assert w.arr.ndim == 3 + len(lead)
            k = w.arr.shape[-2]
            in_specs.append(pl.BlockSpec(none + (None, k, tn), imap(
                lambda i, j, g, lead=lead: lead + (g[i], 0, j))))
        blk += _nbytes((k, tn), w.arr.dtype)
        if w.arr.dtype != BF16:
            temp += _nbytes((k, tn), BF16)
    for arr, kind, off in extras:
        if kind == "mn":
            in_specs.append(pl.BlockSpec((tm, tn), imap(lambda i, j, *pf, off=off: (i, j + off))))
            blk += _nbytes((tm, tn), arr.dtype)
        else:
            in_specs.append(pl.BlockSpec((tm, arr.shape[1]), imap(lambda i, j, *pf: (i, 0))))
            blk += _nbytes((tm, arr.shape[1]), arr.dtype)
    tail = []
    if rowscale is not None:
        assert group is None
        in_specs.append(pl.BlockSpec((tm, V7X_LANES), lambda i, j: (i, 0)))
        blk += _nbytes((tm, V7X_LANES), F32)
        tail.append(rowscale)
    out_specs = [pl.BlockSpec((tm, tn), lambda i, j, *pf: (i, j))]
    out_shape = [jax.ShapeDtypeStruct((m, n), out_dtype)]
    blk += _nbytes((tm, tn), out_dtype)
    if norm_out is not None:
        assert group is None
        in_specs.append(pl.BlockSpec((1, tn), lambda i, j: (0, j)))
        tail.append(norm_out.reshape(1, n).astype(F32))
        out_specs += [pl.BlockSpec((tm, tn), lambda i, j: (i, j)), pl.BlockSpec((tm, V7X_LANES), lambda i, j: (i, 0))]
        out_shape += [jax.ShapeDtypeStruct((m, n), BF16), jax.ShapeDtypeStruct((m, V7X_LANES), F32)]
        blk += _nbytes((tm, tn), BF16) + _nbytes((tm, V7X_LANES), F32)
    inv_k = 1.0 / xs[0].shape[1]
    body = functools.partial(_mm_body, npf=npf, nx=len(xs), ws=tuple(w._replace(arr=None) for w in ws),
                             ne=len(extras), epi=epi, prenorm=prenorm is not None,
                             rowscale=rowscale is not None, norm_out=norm_out is not None, inv_k=inv_k)
    call = pl.pallas_call(
        body,
        grid_spec=pltpu.PrefetchScalarGridSpec(
            num_scalar_prefetch=npf, grid=(gm, gn), in_specs=in_specs, out_specs=out_specs,
            scratch_shapes=scratch),
        out_shape=out_shape,
        compiler_params=pltpu.CompilerParams(
            dimension_semantics=("arbitrary", "arbitrary"), vmem_limit_bytes=_vmem_limit(blk, temp)),
        name=name,
    )
    args = [] if group is None else list(group)
    args += list(xs) + [w.arr for w in ws] + [a for a, _, _ in extras] + tail
    res = call(*args)
    return res if norm_out is not None else res[0]


def _epi_id(accs, ex):
    return accs[0]


def _epi_gelu(accs, ex):
    a = accs[0]
    return 0.5 * a * (1.0 + lax.erf(a * (2.0 ** -0.5)))


def _epi_sigmoid(accs, ex):
    return jax.nn.sigmoid(accs[0])


def _epi_residual(accs, ex):
    return ex[0] + accs[0]


def _epi_swiglu(accs, ex):
    return jax.nn.silu(accs[0]) * accs[1]


def _epi_gated_merge(accs, ex):
    return ex[0].astype(F32) * accs[0] + ex[1].astype(F32) * accs[1]


def _rope_lanes(t, cos_t, sin_a, sin_b):
    half = QK_ROPE // 2
    return (t * cos_t + pltpu.roll(t, half, 1) * sin_a
            + pltpu.roll(t, V7X_LANES - half, 1) * sin_b)


def _epi_rope_all(accs, ex):
    return _rope_lanes(accs[0], *ex)


def _epi_q_heads(accs, ex):
    a = accs[0]
    outs = []
    for g in range(a.shape[1] // V7X_LANES):
        t = a[:, g * V7X_LANES:(g + 1) * V7X_LANES]
        outs.append((_rope_lanes(t, *ex) if g % 2 else t) * Q_PRESCALE)
    return jnp.concatenate(outs, axis=1)


def _rmsnorm_body(x_ref, g_ref, o_ref):
    x = x_ref[...].astype(F32)
    y = x * lax.rsqrt(jnp.mean(x * x, axis=-1, keepdims=True) + NORM_EPS)
    o_ref[...] = (y * g_ref[...]).astype(o_ref.dtype)


def _rmsnorm(x, g, out_dtype, *, width=None, col_block=0, name):
    m = x.shape[0]
    width = x.shape[1] if width is None else width
    tm = _tile(m, 256, 8)
    blk = _nbytes((tm, width), x.dtype) + _nbytes((tm, width), out_dtype)
    return pl.pallas_call(
        _rmsnorm_body,
        grid=(m // tm,),
        in_specs=[pl.BlockSpec((tm, width), lambda i: (i, col_block)),
                  pl.BlockSpec((1, width), lambda i: (0, 0))],
        out_specs=pl.BlockSpec((tm, width), lambda i: (i, 0)),
        out_shape=jax.ShapeDtypeStruct((m, width), out_dtype),
        compiler_params=pltpu.CompilerParams(
            dimension_semantics=("arbitrary",), vmem_limit_bytes=_vmem_limit(blk, 3 * tm * width * 4)),
        name=name,
    )(x, g.reshape(1, width).astype(F32))


def _pack_bf16_pairs(h):
    half = h.shape[1] // 2
    bits = lax.bitcast_convert_type(h.astype(BF16).astype(F32), jnp.uint32)
    return bits[:, :half] | (bits[:, half:] >> 16)


def _unpack_bf16_pairs(p):
    hi = lax.bitcast_convert_type(p & jnp.uint32(0xFFFF0000), F32).astype(BF16)
    lo = lax.bitcast_convert_type(p << 16, F32).astype(BF16)
    return hi, lo


def _norm_router_body(x_ref, g_ref, wr_ref, h_ref, route_ref, *, n_experts):
    x = x_ref[...]
    h = x * lax.rsqrt(jnp.mean(x * x, axis=-1, keepdims=True) + NORM_EPS) * g_ref[...]
    h_ref[...] = _pack_bf16_pairs(h)
    logits = jnp.dot(h, wr_ref[...], preferred_element_type=F32, precision=lax.Precision.HIGHEST)
    lane = lax.broadcasted_iota(jnp.int32, logits.shape, 1).astype(F32)
    neg = jnp.float32(-jnp.inf)
    far = jnp.float32(V7X_LANES)
    l1 = jnp.where(lane < n_experts, logits, neg)
    m1 = jnp.max(l1, axis=-1, keepdims=True)
    i1 = jnp.min(jnp.where(l1 == m1, lane, far), axis=-1, keepdims=True)
    l2 = jnp.where(lane == i1, neg, l1)
    m2 = jnp.max(l2, axis=-1, keepdims=True)
    i2 = jnp.min(jnp.where(l2 == m2, lane, far), axis=-1, keepdims=True)
    e = jnp.exp(m2 - m1)
    g1 = 1.0 / (1.0 + e)
    g2 = e / (1.0 + e)
    route = jnp.where(lane == 0, i1, jnp.where(lane == 1, i2, jnp.where(lane == 2, g1, jnp.where(lane == 3, g2, 0.0))))
    route_ref[...] = route


def _norm_router(x, g, w_router, *, name):
    m, d = x.shape
    n_experts = w_router.shape[1]
    wr = jnp.pad(w_router.astype(F32), ((0, 0), (0, V7X_LANES - n_experts)))
    tm = _tile(m, 256, 8)
    blk = 2 * _nbytes((tm, d), F32) + _nbytes((d, V7X_LANES), F32) + _nbytes((tm, V7X_LANES), F32)
    return pl.pallas_call(
        functools.partial(_norm_router_body, n_experts=n_experts),
        grid=(m // tm,),
        in_specs=[pl.BlockSpec((tm, d), lambda i: (i, 0)),
                  pl.BlockSpec((1, d), lambda i: (0, 0)),
                  pl.BlockSpec((d, V7X_LANES), lambda i: (0, 0))],
        out_specs=[pl.BlockSpec((tm, d // 2), lambda i: (i, 0)),
                   pl.BlockSpec((tm, V7X_LANES), lambda i: (i, 0))],
        out_shape=[jax.ShapeDtypeStruct((m, d // 2), jnp.uint32), jax.ShapeDtypeStruct((m, V7X_LANES), F32)],
        compiler_params=pltpu.CompilerParams(
            dimension_semantics=("arbitrary",), vmem_limit_bytes=_vmem_limit(blk, 4 * tm * d * 4)),
        name=name,
    )(x, g.reshape(1, d).astype(F32), wr)


def _attn_body(q_ref, kn_ref, kr_ref, v_ref, o_ref, kcat_ref, *, sub):
    @pl.when(pl.program_id(2) == 0)
    def _():
        kcat_ref[:, :QK_NOPE] = kn_ref[...]
        kcat_ref[:, QK_NOPE:] = kr_ref[...]

    for r0 in range(0, q_ref.shape[0], sub):
        rows = slice(r0, r0 + sub)
        s = lax.dot_general(q_ref[rows, :], kcat_ref[...], (((1,), (1,)), ((), ())),
                            preferred_element_type=F32)
        p = jnp.exp2(s - jnp.max(s, axis=-1, keepdims=True))
        l = jnp.sum(p, axis=-1, keepdims=True)
        o = jnp.dot(p.astype(BF16), v_ref[...], preferred_element_type=F32)
        o_ref[rows, :] = (o / l).astype(o_ref.dtype)


def _attention(q, kv, k_rope, *, batch, seq, heads, name):
    t = batch * seq
    sub = _tile(seq, 256, 8)
    tq = _tile(seq, 8 * sub, sub)
    nq = seq // tq
    blk = (_nbytes((tq, HEAD_PAD), BF16) + 3 * _nbytes((seq, V7X_LANES), BF16) + _nbytes((tq, V_HEAD), BF16))
    return pl.pallas_call(
        functools.partial(_attn_body, sub=sub),
        grid=(batch, heads, nq),
        in_specs=[pl.BlockSpec((tq, HEAD_PAD), lambda b, h, i: (b * nq + i, h)),
                  pl.BlockSpec((seq, QK_NOPE), lambda b, h, i: (b, 2 * h)),
                  pl.BlockSpec((seq, V7X_LANES), lambda b, h, i: (b, 0)),
                  pl.BlockSpec((seq, V_HEAD), lambda b, h, i: (b, 2 * h + 1))],
        out_specs=pl.BlockSpec((tq, V_HEAD), lambda b, h, i: (b * nq + i, h)),
        out_shape=jax.ShapeDtypeStruct((t, heads * V_HEAD), BF16),
        scratch_shapes=[pltpu.VMEM((seq, HEAD_PAD), BF16)],
        compiler_params=pltpu.CompilerParams(
            dimension_semantics=("arbitrary", "arbitrary", "arbitrary"),
            vmem_limit_bytes=_vmem_limit(blk, _nbytes((seq, HEAD_PAD), BF16) + 4 * tq * seq * 4)),
        name=name,
    )(q, kv, k_rope, kv)


def _gmlp_body(u_ref, v_ref, g_ref, b_ref, w_ref, bs_ref, o_ref, *, n_chunks, n_groups):
    v = v_ref[...].astype(F32)
    mu = jnp.mean(v, axis=-1, keepdims=True)
    vc = v - mu
    vn = vc * lax.rsqrt(jnp.mean(vc * vc, axis=-1, keepdims=True) + NORM_EPS)
    vn = (vn * g_ref[...] + b_ref[...]).astype(BF16)
    for c in range(n_chunks):
        rows = slice(c * CHUNK, (c + 1) * CHUNK)
        for g in range(n_groups):
            cols = slice(g * GROUP_DIM, (g + 1) * GROUP_DIM)
            s = jnp.dot(w_ref[g], vn[rows, cols], preferred_element_type=F32) + bs_ref[g]
            o_ref[rows, cols] = (u_ref[rows, cols].astype(F32) * s).astype(o_ref.dtype)


def _gmlp(uv, v_norm_g, v_norm_b, w_sp, b_sp, *, name):
    t = uv.shape[0]
    gw = uv.shape[1] // 2
    n_groups = w_sp.shape[0]
    rows = _tile(t, 2 * CHUNK, CHUNK)
    b_full = jnp.broadcast_to(b_sp.astype(F32)[:, :, None], (n_groups, CHUNK, GROUP_DIM))
    blk = 3 * _nbytes((rows, gw), BF16) + _nbytes(w_sp.shape, BF16) + _nbytes(b_full.shape, F32)
    return pl.pallas_call(
        functools.partial(_gmlp_body, n_chunks=rows // CHUNK, n_groups=n_groups),
        grid=(t // rows,),
        in_specs=[pl.BlockSpec((rows, gw), lambda i: (i, 0)),
                  pl.BlockSpec((rows, gw), lambda i: (i, 1)),
                  pl.BlockSpec((1, gw), lambda i: (0, 0)),
                  pl.BlockSpec((1, gw), lambda i: (0, 0)),
                  pl.BlockSpec((n_groups, CHUNK, CHUNK), lambda i: (0, 0, 0)),
                  pl.BlockSpec((n_groups, CHUNK, GROUP_DIM), lambda i: (0, 0, 0))],
        out_specs=pl.BlockSpec((rows, gw), lambda i: (i, 0)),
        out_shape=jax.ShapeDtypeStruct((t, gw), BF16),
        compiler_params=pltpu.CompilerParams(
            dimension_semantics=("arbitrary",), vmem_limit_bytes=_vmem_limit(blk, 4 * rows * gw * 4)),
        name=name,
    )(uv, uv, v_norm_g.reshape(1, gw).astype(F32), v_norm_b.reshape(1, gw).astype(F32),
      w_sp.astype(BF16), b_full)


def _row_copy(src_hbm, dst_vmem, src_row, dst_row, sem):
    return pltpu.make_async_copy(src_hbm.at[pl.ds(src_row, 1)], dst_vmem.at[pl.ds(dst_row, 1)], sem)


def _moe_up_body(exp_ref, nact_ref, rows_ref, tok_ref, hp_hbm, w1_ref, w3_ref, o_ref, xbuf, xbf, sem, *, tile, gn):
    r = pl.program_id(0)
    j = pl.program_id(1)
    n_active = nact_ref[0]
    chunk = tile // gn
    half = xbuf.shape[2]

    def issue(row_tile, slot, row0, n_rows):
        base = row_tile * tile

        def f(k, c):
            row = row0 + k
            _row_copy(hp_hbm, xbuf.at[slot], tok_ref[base + row], row, sem.at[slot]).start()
            return c

        lax.fori_loop(0, n_rows, f, 0, unroll=8)

    @pl.when(jnp.logical_and(r == 0, j == 0))
    def _():
        issue(0, 0, 0, tile)

    @pl.when(jnp.logical_and(r < n_active, j == 0))
    def _():
        slot = r % 2

        def drain(k, c):
            _row_copy(hp_hbm, xbuf.at[slot], 0, k, sem.at[slot]).wait()
            return c

        lax.fori_loop(0, tile, drain, 0, unroll=8)
        hi, lo = _unpack_bf16_pairs(xbuf[slot])
        xbf[:, :half] = hi
        xbf[:, half:] = lo

    def compute(nrows):
        x = xbf[:nrows, :]
        a = jnp.dot(x, w1_ref[...].astype(BF16), preferred_element_type=F32)
        b = jnp.dot(x, w3_ref[...].astype(BF16), preferred_element_type=F32)
        o_ref[:nrows, :] = (jax.nn.silu(a) * b).astype(o_ref.dtype)
        if nrows < tile:
            o_ref[nrows:, :] = jnp.zeros((tile - nrows, o_ref.shape[1]), o_ref.dtype)

    rows = rows_ref[r]
    pl.when(rows > tile // 2)(functools.partial(compute, tile))
    pl.when(jnp.logical_and(rows > 0, rows <= tile // 2))(functools.partial(compute, tile // 2))

    @pl.when(rows == 0)
    def _():
        o_ref[...] = jnp.zeros_like(o_ref)

    @pl.when(r + 1 < n_active)
    def _():
        issue(r + 1, (r + 1) % 2, j * chunk, chunk)


def _moe_up(hp, slot_tok, group, w1, w3, lead, *, tile, tn, name):
    tile_exp, n_active, tile_rows = group
    n_slots = slot_tok.shape[0]
    half = hp.shape[1]
    d, f = w1.shape[-2], w1.shape[-1]
    gn = f // tn
    assert tile % gn == 0

    def w_map(r, j, e, na, rows, tok):
        re, je = _active_ij(r, j, na[0], gn)
        return (lead, e[re], 0, je)

    blk = 2 * _nbytes((d, tn), w1.dtype) + _nbytes((tile, tn), BF16)
    temp = (2 * _nbytes((tile, half), jnp.uint32) + _nbytes((tile, d), BF16)
            + 2 * _nbytes((d, tn), BF16) + 4 * tile * tn * 4)
    return pl.pallas_call(
        functools.partial(_moe_up_body, tile=tile, gn=gn),
        grid_spec=pltpu.PrefetchScalarGridSpec(
            num_scalar_prefetch=4, grid=(n_slots // tile, gn),
            in_specs=[pl.BlockSpec(memory_space=pl.ANY),
                      pl.BlockSpec((None, None, d, tn), w_map),
                      pl.BlockSpec((None, None, d, tn), w_map)],
            out_specs=pl.BlockSpec((tile, tn), lambda r, j, *pf: (r, j)),
            scratch_shapes=[pltpu.VMEM((2, tile, half), jnp.uint32), pltpu.VMEM((tile, d), BF16),
                            pltpu.SemaphoreType.DMA((2,))]),
        out_shape=jax.ShapeDtypeStruct((n_slots, f), BF16),
        compiler_params=pltpu.CompilerParams(
            dimension_semantics=("arbitrary", "arbitrary"), vmem_limit_bytes=_vmem_limit(blk, temp)),
        name=name,
    )(tile_exp, n_active, tile_rows, slot_tok, hp, w1, w3)


def _combine_body(slot_ref, x_ref, route_ref, g_ref, y_hbm, o_ref, buf, sem, *, rows, n_tiles, norm):
    i = pl.program_id(0)

    def issue(tile, slot):
        base = tile * rows

        def f(k, c):
            for j in range(TOP_K):
                _row_copy(y_hbm, buf.at[slot, j], slot_ref[(base + k) * TOP_K + j], k, sem.at[slot]).start()
            return c

        lax.fori_loop(0, rows, f, 0, unroll=4)

    @pl.when(i == 0)
    def _():
        issue(0, 0)

    @pl.when(i + 1 < n_tiles)
    def _():
        issue(i + 1, (i + 1) % 2)

    slot = i % 2

    def drain(k, c):
        for j in range(TOP_K):
            _row_copy(y_hbm, buf.at[slot, j], 0, k, sem.at[slot]).wait()
        return c

    lax.fori_loop(0, rows, drain, 0, unroll=4)
    route = route_ref[...]
    y = x_ref[...] + (buf[slot, 0] * route[:, 2:3] + buf[slot, 1] * route[:, 3:4])
    if norm:
        y = y * lax.rsqrt(jnp.mean(y * y, axis=-1, keepdims=True) + NORM_EPS) * g_ref[...]
    o_ref[...] = y


def _combine(x, route, slot_of_assign, yb, norm_gain, *, name):
    t, d = x.shape
    rows = _tile(t, 256, 8)
    n_tiles = t // rows
    norm = norm_gain is not None
    g = (norm_gain if norm else jnp.ones((d,), F32)).reshape(1, d).astype(F32)
    blk = 2 * _nbytes((rows, d), F32) + _nbytes((rows, V7X_LANES), F32)
    return pl.pallas_call(
        functools.partial(_combine_body, rows=rows, n_tiles=n_tiles, norm=norm),
        grid_spec=pltpu.PrefetchScalarGridSpec(
            num_scalar_prefetch=1, grid=(n_tiles,),
            in_specs=[pl.BlockSpec((rows, d), lambda i, s: (i, 0)),
                      pl.BlockSpec((rows, V7X_LANES), lambda i, s: (i, 0)),
                      pl.BlockSpec((1, d), lambda i, s: (0, 0)),
                      pl.BlockSpec(memory_space=pl.ANY)],
            out_specs=pl.BlockSpec((rows, d), lambda i, s: (i, 0)),
            scratch_shapes=[pltpu.VMEM((2, TOP_K, rows, d), F32), pltpu.SemaphoreType.DMA((2,))]),
        out_shape=jax.ShapeDtypeStruct((t, d), F32),
        compiler_params=pltpu.CompilerParams(
            dimension_semantics=("arbitrary",),
            vmem_limit_bytes=_vmem_limit(blk, (2 * TOP_K + 3) * _nbytes((rows, d), F32))),
        name=name,
    )(slot_of_assign, x, route, g, yb)


def _routing_tables(route, n_experts, tile):
    t = route.shape[0]
    a = t * TOP_K
    flat_e = route[:, :TOP_K].astype(jnp.int32).reshape(a)
    order = jnp.argsort(flat_e).astype(jnp.int32)
    rank = jnp.argsort(order).astype(jnp.int32)
    counts = jnp.sum(flat_e[:, None] == jnp.arange(n_experts, dtype=jnp.int32)[None, :], axis=0, dtype=jnp.int32)
    padded = (counts + tile - 1) // tile * tile
    pad_end = jnp.cumsum(padded)
    pad_start = pad_end - padded
    shift = pad_start - (jnp.cumsum(counts) - counts)
    slot_of_assign = rank + shift[flat_e]
    n_tiles = -(-a // tile) + n_experts
    tile_start = jnp.arange(n_tiles, dtype=jnp.int32) * tile
    tile_exp = jnp.minimum(jnp.searchsorted(pad_end, tile_start, side="right"),
                           n_experts - 1).astype(jnp.int32)
    slot = jnp.arange(n_tiles * tile, dtype=jnp.int32)
    slot_exp = jnp.repeat(tile_exp, tile)
    real = slot - pad_start[slot_exp] < counts[slot_exp]
    slot_tok = jnp.where(real, order[jnp.clip(slot - shift[slot_exp], 0, a - 1)] // TOP_K, 0)
    n_active = (pad_end[-1:] // tile).astype(jnp.int32)
    tile_rows = jnp.clip(counts[tile_exp] - (tile_start - pad_start[tile_exp]), 0, tile).astype(jnp.int32)
    return slot_tok.astype(jnp.int32), slot_of_assign.astype(jnp.int32), (tile_exp, n_active, tile_rows)


def _rope_panels(positions):
    inv_freq = ROPE_THETA ** (-jnp.arange(0, QK_ROPE, 2, dtype=F32) / QK_ROPE)
    ang = positions.astype(F32).reshape(-1, 1) * inv_freq
    cos, sin = jnp.cos(ang), jnp.sin(ang)
    z = jnp.zeros_like(cos)
    pad = jnp.zeros((cos.shape[0], V7X_LANES - QK_ROPE), F32)
    cos_t = jnp.concatenate([cos, cos, pad], axis=1)
    sin_a = jnp.concatenate([z, sin, pad], axis=1)
    sin_b = jnp.concatenate([-sin, z, pad], axis=1)
    return cos_t, sin_a, sin_b


def kernel(x, positions, mix_norm, w_in, q_norm, kv_norm, w_uq, w_ukv, v_norm_g, v_norm_b, w_sp, b_sp, w_branch_a, w_branch_b, w_out, ffn_norm, dense_w1, dense_w3, dense_w2, w_router, moe_w1, moe_w3, moe_w2, final_norm):
    batch, seq, d = x.shape
    t = batch * seq
    depth = mix_norm.shape[0]
    ql, kvl = q_norm.shape[1], kv_norm.shape[1]
    heads = w_ukv.shape[2] // (QK_NOPE + V_HEAD)
    gw = v_norm_g.shape[1]
    o2 = ql + kvl
    o3 = o2 + QK_ROPE
    o4 = o3 + 2 * gw
    assert ql % kvl == 0
    tm = _tile(t, 1024, 8)
    tn_d = _tile(d, 512)
    w_in_t = jnp.swapaxes(w_in, 1, 2)

    rope = _rope_panels(positions)
    rope_ex = [(p, "m", 0) for p in rope]
    xf = x.reshape(t, d)
    out = h = h_ss = None
    for layer in range(depth):
        w_q = jnp.pad(w_uq[layer].reshape(ql, heads, QK_NOPE + QK_ROPE),
                      ((0, 0), (0, 0), (0, HEAD_PAD - QK_NOPE - QK_ROPE))).reshape(ql, heads * HEAD_PAD).astype(BF16)
        w_kv = w_ukv[layer].astype(BF16)

        if h is None:
            h, h_ss = _rmsnorm(xf, mix_norm[layer], BF16, name=f"mix_norm{layer}"), None
        cqkv = _mm([h], [W(0, w_in_t, layer, nt=True)], _epi_id, [], o2, F32,
                   tm=tm, tn=_tile(o2, 512), rowscale=h_ss, name=f"in_latent{layer}")
        k_rope = _mm([h], [W(0, w_in_t, layer, nt=True, row0=o2)], _epi_rope_all, rope_ex, V7X_LANES, BF16,
                     tm=tm, tn=V7X_LANES, rowscale=h_ss, name=f"in_krope{layer}")
        uv = _mm([h], [W(0, w_in_t, layer, nt=True, row0=o3)], _epi_gelu, [], 2 * gw, BF16,
                 tm=tm, tn=_tile(2 * gw, 512), rowscale=h_ss, name=f"in_uv{layer}")
        gates = _mm([h], [W(0, w_in_t, layer, nt=True, row0=o4)], _epi_sigmoid, [], 2 * d, BF16,
                    tm=tm, tn=_tile(2 * d, 512), rowscale=h_ss, name=f"in_gate{layer}")
        h = None

        q = _mm([cqkv], [W(0, w_q)], _epi_q_heads, rope_ex, heads * HEAD_PAD, BF16,
                tm=tm, tn=_tile(heads * HEAD_PAD, 512, 2 * V7X_LANES),
                prenorm=(q_norm[layer], ql, 0), name=f"q_up{layer}")
        kv = _mm([cqkv], [W(0, w_kv)], _epi_id, [], heads * (QK_NOPE + V_HEAD), BF16,
                 tm=tm, tn=_tile(heads * (QK_NOPE + V_HEAD), 512),
                 prenorm=(kv_norm[layer], kvl, ql // kvl), name=f"kv_up{layer}")
        y_a = _attention(q, kv, k_rope, batch=batch, seq=seq, heads=heads, name=f"attention{layer}")
        y_b = _gmlp(uv, v_norm_g[layer], v_norm_b[layer], w_sp[layer], b_sp[layer], name=f"gmlp{layer}")

        merged = _mm([y_a, y_b], [W(0, w_branch_a, layer), W(1, w_branch_b, layer)],
                     _epi_gated_merge, [(gates, "mn", 0), (gates, "mn", d // tn_d)], d, BF16,
                     tm=tm, tn=tn_d, name=f"merge{layer}")
        i = layer // 2
        last = layer == depth - 1
        dense = layer % 2 == 0
        xf = _mm([merged], [W(0, w_out, layer)], _epi_residual, [(xf, "mn", 0)], d, F32,
                 tm=tm, tn=tn_d, norm_out=ffn_norm[layer] if dense else None, name=f"mix_out{layer}")
        if dense:
            ff = dense_w1.shape[2]
            xf, hg, hg_ss = xf
            act = _mm([hg], [W(0, dense_w1, i), W(0, dense_w3, i)], _epi_swiglu, [], ff, BF16,
                      tm=tm, tn=_tile(ff, 256), rowscale=hg_ss, name=f"dense_up{layer}")
            xf = _mm([act], [W(0, dense_w2[i].astype(BF16))], _epi_residual, [(xf, "mn", 0)], d, F32,
                     tm=_tile(t, 512, 8), tn=_tile(d, 512),
                     norm_out=None if last else mix_norm[layer + 1], name=f"dense_down{layer}")
            if last:
                out = _rmsnorm(xf, final_norm, F32, name="final_norm")
            else:
                xf, h, h_ss = xf
        else:
            n_experts = w_router.shape[2]
            fe = moe_w1.shape[3]
            hp, route = _norm_router(xf, ffn_norm[layer], w_router[i], name=f"ffn_norm_router{layer}")
            slot_tok, slot_of_assign, group = _routing_tables(route, n_experts, MOE_TILE)
            act = _moe_up(hp, slot_tok, group, moe_w1, moe_w3, i,
                          tile=MOE_TILE, tn=_tile(fe, 256), name=f"moe_up{layer}")
            yb = _mm([act], [W(0, moe_w2, i)], _epi_id, [], d, F32,
                     tm=MOE_TILE, tn=tn_d, group=group, name=f"moe_down{layer}")
            xf = _combine(xf, route, slot_of_assign, yb, final_norm if last else None, name=f"moe_combine{layer}")
            if last:
                out = xf
    return out.reshape(batch, seq, d)
```

```python
import functools
import math
from typing import Any, NamedTuple

import jax
import jax.numpy as jnp
from jax import lax
from jax.experimental import pallas as pl
from jax.experimental.pallas import tpu as pltpu

F32 = jnp.float32
BF16 = jnp.bfloat16

V7X_LANES = 128
V7X_VMEM_BYTES = 64 * 1024 * 1024
V7X_VMEM_CAP = V7X_VMEM_BYTES - 6 * 1024 * 1024

QK_NOPE = 128
QK_ROPE = 64
V_HEAD = 128
HEAD_PAD = 256
GROUP_DIM = 128
CHUNK = 128
TOP_K = 2
MOE_TILE = 1024
NORM_EPS = 1e-6
ROPE_THETA = 10000.0
Q_PRESCALE = float(QK_NOPE + QK_ROPE) ** -0.5 * math.log2(math.e)


def _tile(n, pref, unit=V7X_LANES):
    t = (min(pref, n) // unit) * unit
    while t >= unit:
        if n % t == 0:
            return t
        t -= unit
    return n


def _vmem_limit(block_bytes, temp_bytes):
    return int(min(V7X_VMEM_CAP, 2 * block_bytes + temp_bytes + (4 << 20)))


def _nbytes(shape, dtype):
    n = 1
    for s in shape:
        n *= s
    return n * jnp.dtype(dtype).itemsize


class W(NamedTuple):
    xi: int
    arr: Any
    lead: Any = None
    off: int = 0
    nt: bool = False
    row0: int = 0


_NT_DIMS = (((1,), (1,)), ((), ()))


def _mm_compute(x_refs, w_refs, e_refs, ss_ref, gain_ref, o_refs, ws, epi, inv_k, nrows=None):
    if nrows is not None:
        assert not e_refs and ss_ref is None and gain_ref is None
        o_ref = o_refs[0]
        x_refs = [x.at[pl.ds(0, nrows)] for x in x_refs]
        o_refs = [o_ref.at[pl.ds(0, nrows)]]
        o_ref[nrows:, :] = jnp.zeros((o_ref.shape[0] - nrows, o_ref.shape[1]), o_ref.dtype)
    accs = []
    for w, wr in zip(ws, w_refs):
        x = x_refs[w.xi][...]
        mat = wr[...].astype(BF16)
        if w.nt:
            accs.append(lax.dot_general(x, mat, _NT_DIMS, preferred_element_type=F32))
        else:
            accs.append(jnp.dot(x, mat, preferred_element_type=F32))
    if ss_ref is not None:
        r = lax.rsqrt(ss_ref[:, :1] * inv_k + NORM_EPS)
        accs = [a * r for a in accs]
    res = epi(accs, [e[...] for e in e_refs])
    o_refs[0][...] = res.astype(o_refs[0].dtype)
    if gain_ref is not None:
        xg_ref, sso_ref = o_refs[1], o_refs[2]
        xg_ref[...] = (res * gain_ref[...]).astype(xg_ref.dtype)
        part = jnp.broadcast_to(jnp.sum(res * res, axis=-1, keepdims=True), sso_ref.shape)
        j = pl.program_id(1)

        @pl.when(j == 0)
        def _():
            sso_ref[...] = part

        @pl.when(j > 0)
        def _():
            sso_ref[...] += part


def _mm_body(*refs, npf, nx, ws, ne, epi, prenorm, rowscale, norm_out, inv_k):
    pf = refs[:npf]
    refs = refs[npf:]
    x_refs, refs = refs[:nx], refs[nx:]
    w_refs, refs = refs[:len(ws)], refs[len(ws):]
    e_refs, refs = refs[:ne], refs[ne:]
    ss_ref = gain_ref = None
    if rowscale:
        ss_ref, refs = refs[0], refs[1:]
    if norm_out:
        gain_ref, refs = refs[0], refs[1:]
    n_out = 3 if norm_out else 1
    o_refs, refs = refs[:n_out], refs[n_out:]
    if prenorm:
        xn_ref = refs[0]

        @pl.when(pl.program_id(1) == 0)
        def _():
            _rmsnorm_body(x_refs[0], x_refs[1], xn_ref)

        x_refs = (xn_ref,)
    compute = functools.partial(_mm_compute, x_refs, w_refs, e_refs, ss_ref, gain_ref, o_refs, ws, epi, inv_k)
    if npf:
        rows = pf[2][pl.program_id(0)]
        half = o_refs[0].shape[0] // 2
        pl.when(rows > half)(compute)
        pl.when(jnp.logical_and(rows > 0, rows <= half))(functools.partial(compute, nrows=half))

        @pl.when(rows == 0)
        def _():
            o_refs[0][...] = jnp.zeros_like(o_refs[0])
    else:
        compute()


def _active_ij(i, j, n_active, gn):
    return jnp.minimum(i, n_active - 1), jnp.where(i < n_active, j, gn - 1)


def _mm(xs, ws, epi, extras, n, out_dtype, *, tm, tn, group=None, prenorm=None, rowscale=None, norm_out=None,
        name):
    m = xs[0].shape[0]
    gm, gn = m // tm, n // tn
    npf = 0 if group is None else 3

    def imap(f):
        if group is None:
            return f
        return lambda i, j, g, na, rows: f(*_active_ij(i, j, na[0], gn), g)

    in_specs = []
    blk = 0
    temp = (len(ws) + 2) * tm * tn * 4
    scratch = []
    if prenorm is None:
        for x in xs:
            in_specs.append(pl.BlockSpec((tm, x.shape[1]), imap(lambda i, j, *pf: (i, 0))))
            blk += _nbytes((tm, x.shape[1]), x.dtype)
    else:
        gain, width, col_block = prenorm
        assert len(xs) == 1 and group is None
        in_specs.append(pl.BlockSpec((tm, width), lambda i, j: (i, col_block)))
        in_specs.append(pl.BlockSpec((1, width), lambda i, j: (0, 0)))
        xs = [xs[0], gain.reshape(1, width).astype(F32)]
        blk += _nbytes((tm, width), F32)
        scratch.append(pltpu.VMEM((tm, width), BF16))
        temp += _nbytes((tm, width), BF16) + 2 * _nbytes((tm, width), F32)
    for w in ws:
        lead = () if w.lead is None else (w.lead,)
        none = (None,) * len(lead)
        if w.nt:
            assert group is None and w.arr.ndim == 3
            k = w.arr.shape[-1]
            in_specs.append(pl.BlockSpec((pl.Squeezed(), pl.Element(tn), pl.Element(k)), imap(
                lambda i, j, *pf, lead=w.lead, row0=w.row0: (lead, pl.multiple_of(row0 + j * tn, 8), 0))))
        elif group is None:
            assert w.arr.ndim == 2 + len(lead)
            k = w.arr.shape[-2]
            in_specs.append(pl.BlockSpec(none + (k, tn), imap(
                lambda i, j, *pf, lead=lead, off=w.off: lead + (0, j + off))))
        else:
            assert w.arr.ndim == 3 + len(lead)
            k = w.arr.shape[-2]
            in_specs.append(pl.BlockSpec(none + (None, k, tn), imap(
                lambda i, j, g, lead=lead: lead + (g[i], 0, j))))
        blk += _nbytes((k, tn), w.arr.dtype)
        if w.arr.dtype != BF16:
            temp += _nbytes((k, tn), BF16)
    for arr, kind, off in extras:
        if kind == "mn":
            in_specs.append(pl.BlockSpec((tm, tn), imap(lambda i, j, *pf, off=off: (i, j + off))))
            blk += _nbytes((tm, tn), arr.dtype)
        else:
            in_specs.append(pl.BlockSpec((tm, arr.shape[1]), imap(lambda i, j, *pf: (i, 0))))
            blk += _nbytes((tm, arr.shape[1]), arr.dtype)
    tail = []
    if rowscale is not None:
        assert group is None
        in_specs.append(pl.BlockSpec((tm, V7X_LANES), lambda i, j: (i, 0)))
        blk += _nbytes((tm, V7X_LANES), F32)
        tail.append(rowscale)
    out_specs = [pl.BlockSpec((tm, tn), lambda i, j, *pf: (i, j))]
    out_shape = [jax.ShapeDtypeStruct((m, n), out_dtype)]
    blk += _nbytes((tm, tn), out_dtype)
    if norm_out is not None:
        assert group is None
        in_specs.append(pl.BlockSpec((1, tn), lambda i, j: (0, j)))
        tail.append(norm_out.reshape(1, n).astype(F32))
        out_specs += [pl.BlockSpec((tm, tn), lambda i, j: (i, j)), pl.BlockSpec((tm, V7X_LANES), lambda i, j: (i, 0))]
        out_shape += [jax.ShapeDtypeStruct((m, n), BF16), jax.ShapeDtypeStruct((m, V7X_LANES), F32)]
        blk += _nbytes((tm, tn), BF16) + _nbytes((tm, V7X_LANES), F32)
    inv_k = 1.0 / xs[0].shape[1]
    body = functools.partial(_mm_body, npf=npf, nx=len(xs), ws=tuple(w._replace(arr=None) for w in ws),
                             ne=len(extras), epi=epi, prenorm=prenorm is not None,
                             rowscale=rowscale is not None, norm_out=norm_out is not None, inv_k=inv_k)
    call = pl.pallas_call(
        body,
        grid_spec=pltpu.PrefetchScalarGridSpec(
            num_scalar_prefetch=npf, grid=(gm, gn), in_specs=in_specs, out_specs=out_specs,
            scratch_shapes=scratch),
        out_shape=out_shape,
        compiler_params=pltpu.CompilerParams(
            dimension_semantics=("arbitrary", "arbitrary"), vmem_limit_bytes=_vmem_limit(blk, temp)),
        name=name,
    )
    args = [] if group is None else list(group)
    args += list(xs) + [w.arr for w in ws] + [a for a, _, _ in extras] + tail
    res = call(*args)
    return res if norm_out is not None else res[0]


def _epi_id(accs, ex):
    return accs[0]


def _epi_gelu(accs, ex):
    a = accs[0]
    return 0.5 * a * (1.0 + lax.erf(a * (2.0 ** -0.5)))


def _epi_sigmoid(accs, ex):
    return jax.nn.sigmoid(accs[0])


def _epi_residual(accs, ex):
    return ex[0] + accs[0]


def _epi_swiglu(accs, ex):
    return jax.nn.silu(accs[0]) * accs[1]


def _epi_gated_merge(accs, ex):
    return ex[0].astype(F32) * accs[0] + ex[1].astype(F32) * accs[1]


def _rope_lanes(t, cos_t, sin_a, sin_b):
    half = QK_ROPE // 2
    return (t * cos_t + pltpu.roll(t, half, 1) * sin_a
            + pltpu.roll(t, V7X_LANES - half, 1) * sin_b)


def _epi_rope_all(accs, ex):
    return _rope_lanes(accs[0], *ex)


def _epi_q_heads(accs, ex):
    a = accs[0]
    outs = []
    for g in range(a.shape[1] // V7X_LANES):
        t = a[:, g * V7X_LANES:(g + 1) * V7X_LANES]
        outs.append((_rope_lanes(t, *ex) if g % 2 else t) * Q_PRESCALE)
    return jnp.concatenate(outs, axis=1)


def _rmsnorm_body(x_ref, g_ref, o_ref):
    x = x_ref[...].astype(F32)
    y = x * lax.rsqrt(jnp.mean(x * x, axis=-1, keepdims=True) + NORM_EPS)
    o_ref[...] = (y * g_ref[...]).astype(o_ref.dtype)


def _rmsnorm(x, g, out_dtype, *, width=None, col_block=0, name):
    m = x.shape[0]
    width = x.shape[1] if width is None else width
    tm = _tile(m, 256, 8)
    blk = _nbytes((tm, width), x.dtype) + _nbytes((tm, width), out_dtype)
    return pl.pallas_call(
        _rmsnorm_body,
        grid=(m // tm,),
        in_specs=[pl.BlockSpec((tm, width), lambda i: (i, col_block)),
                  pl.BlockSpec((1, width), lambda i: (0, 0))],
        out_specs=pl.BlockSpec((tm, width), lambda i: (i, 0)),
        out_shape=jax.ShapeDtypeStruct((m, width), out_dtype),
        compiler_params=pltpu.CompilerParams(
            dimension_semantics=("arbitrary",), vmem_limit_bytes=_vmem_limit(blk, 3 * tm * width * 4)),
        name=name,
    )(x, g.reshape(1, width).astype(F32))


def _pack_bf16_pairs(h):
    half = h.shape[1] // 2
    bits = lax.bitcast_convert_type(h.astype(BF16).astype(F32), jnp.uint32)
    return bits[:, :half] | (bits[:, half:] >> 16)


def _unpack_bf16_pairs(p):
    hi = lax.bitcast_convert_type(p & jnp.uint32(0xFFFF0000), F32).astype(BF16)
    lo = lax.bitcast_convert_type(p << 16, F32).astype(BF16)
    return hi, lo


def _norm_router_body(x_ref, g_ref, wr_ref, h_ref, route_ref, *, n_experts):
    x = x_ref[...]
    h = x * lax.rsqrt(jnp.mean(x * x, axis=-1, keepdims=True) + NORM_EPS) * g_ref[...]
    h_ref[...] = _pack_bf16_pairs(h)
    wr = wr_ref[...]
    h_hi, w_hi = h.astype(BF16), wr.astype(BF16)
    h_lo = (h - h_hi.astype(F32)).astype(BF16)
    w_lo = (wr - w_hi.astype(F32)).astype(BF16)
    logits = (jnp.dot(h_hi, w_hi, preferred_element_type=F32)
              + (jnp.dot(h_lo, w_hi, preferred_element_type=F32) + jnp.dot(h_hi, w_lo, preferred_element_type=F32)))
    lane = lax.broadcasted_iota(jnp.int32, logits.shape, 1).astype(F32)
    neg = jnp.float32(-jnp.inf)
    far = jnp.float32(V7X_LANES)
    l1 = jnp.where(lane < n_experts, logits, neg)
    m1 = jnp.max(l1, axis=-1, keepdims=True)
    i1 = jnp.min(jnp.where(l1 == m1, lane, far), axis=-1, keepdims=True)
    l2 = jnp.where(lane == i1, neg, l1)
    m2 = jnp.max(l2, axis=-1, keepdims=True)
    i2 = jnp.min(jnp.where(l2 == m2, lane, far), axis=-1, keepdims=True)
    e = jnp.exp(m2 - m1)
    g1 = 1.0 / (1.0 + e)
    g2 = e / (1.0 + e)
    route = jnp.where(lane == 0, i1, jnp.where(lane == 1, i2, jnp.where(lane == 2, g1, jnp.where(lane == 3, g2, 0.0))))
    route_ref[...] = route


def _norm_router(x, g, w_router, *, name):
    m, d = x.shape
    n_experts = w_router.shape[1]
    wr = jnp.pad(w_router.astype(F32), ((0, 0), (0, V7X_LANES - n_experts)))
    tm = _tile(m, 256, 8)
    blk = 2 * _nbytes((tm, d), F32) + _nbytes((d, V7X_LANES), F32) + _nbytes((tm, V7X_LANES), F32)
    return pl.pallas_call(
        functools.partial(_norm_router_body, n_experts=n_experts),
        grid=(m // tm,),
        in_specs=[pl.BlockSpec((tm, d), lambda i: (i, 0)),
                  pl.BlockSpec((1, d), lambda i: (0, 0)),
                  pl.BlockSpec((d, V7X_LANES), lambda i: (0, 0))],
        out_specs=[pl.BlockSpec((tm, d // 2), lambda i: (i, 0)),
                   pl.BlockSpec((tm, V7X_LANES), lambda i: (i, 0))],
        out_shape=[jax.ShapeDtypeStruct((m, d // 2), jnp.uint32), jax.ShapeDtypeStruct((m, V7X_LANES), F32)],
        compiler_params=pltpu.CompilerParams(
            dimension_semantics=("arbitrary",), vmem_limit_bytes=_vmem_limit(blk, 4 * tm * d * 4)),
        name=name,
    )(x, g.reshape(1, d).astype(F32), wr)


def _attn_body(q_ref, kn_ref, kr_ref, v_ref, o_ref, kcat_ref, *, sub):
    @pl.when(pl.program_id(2) == 0)
    def _():
        kcat_ref[:, :QK_NOPE] = kn_ref[...]
        kcat_ref[:, QK_NOPE:] = kr_ref[...]

    for r0 in range(0, q_ref.shape[0], sub):
        rows = slice(r0, r0 + sub)
        s = lax.dot_general(q_ref[rows, :], kcat_ref[...], (((1,), (1,)), ((), ())),
                            preferred_element_type=F32)
        p = jnp.exp2(s - jnp.max(s, axis=-1, keepdims=True))
        l = jnp.sum(p, axis=-1, keepdims=True)
        o = jnp.dot(p.astype(BF16), v_ref[...], preferred_element_type=F32)
        o_ref[rows, :] = (o / l).astype(o_ref.dtype)


def _attention(q, kv, k_rope, *, batch, seq, heads, name):
    t = batch * seq
    sub = _tile(seq, 256, 8)
    tq = _tile(seq, 8 * sub, sub)
    nq = seq // tq
    blk = (_nbytes((tq, HEAD_PAD), BF16) + 3 * _nbytes((seq, V7X_LANES), BF16) + _nbytes((tq, V_HEAD), BF16))
    return pl.pallas_call(
        functools.partial(_attn_body, sub=sub),
        grid=(batch, heads, nq),
        in_specs=[pl.BlockSpec((tq, HEAD_PAD), lambda b, h, i: (b * nq + i, h)),
                  pl.BlockSpec((seq, QK_NOPE), lambda b, h, i: (b, 2 * h)),
                  pl.BlockSpec((seq, V7X_LANES), lambda b, h, i: (b, 0)),
                  pl.BlockSpec((seq, V_HEAD), lambda b, h, i: (b, 2 * h + 1))],
        out_specs=pl.BlockSpec((tq, V_HEAD), lambda b, h, i: (b * nq + i, h)),
        out_shape=jax.ShapeDtypeStruct((t, heads * V_HEAD), BF16),
        scratch_shapes=[pltpu.VMEM((seq, HEAD_PAD), BF16)],
        compiler_params=pltpu.CompilerParams(
            dimension_semantics=("arbitrary", "arbitrary", "arbitrary"),
            vmem_limit_bytes=_vmem_limit(blk, _nbytes((seq, HEAD_PAD), BF16) + 4 * tq * seq * 4)),
        name=name,
    )(q, kv, k_rope, kv)


def _gmlp_body(u_ref, v_ref, g_ref, b_ref, w_ref, bs_ref, o_ref, *, n_chunks, n_groups):
    v = v_ref[...].astype(F32)
    mu = jnp.mean(v, axis=-1, keepdims=True)
    vc = v - mu
    vn = vc * lax.rsqrt(jnp.mean(vc * vc, axis=-1, keepdims=True) + NORM_EPS)
    vn = (vn * g_ref[...] + b_ref[...]).astype(BF16)
    for c in range(n_chunks):
        rows = slice(c * CHUNK, (c + 1) * CHUNK)
        for g in range(n_groups):
            cols = slice(g * GROUP_DIM, (g + 1) * GROUP_DIM)
            s = jnp.dot(w_ref[g], vn[rows, cols], preferred_element_type=F32) + bs_ref[g]
            o_ref[rows, cols] = (u_ref[rows, cols].astype(F32) * s).astype(o_ref.dtype)


def _gmlp(uv, v_norm_g, v_norm_b, w_sp, b_sp, *, name):
    t = uv.shape[0]
    gw = uv.shape[1] // 2
    n_groups = w_sp.shape[0]
    rows = _tile(t, 2 * CHUNK, CHUNK)
    b_full = jnp.broadcast_to(b_sp.astype(F32)[:, :, None], (n_groups, CHUNK, GROUP_DIM))
    blk = 3 * _nbytes((rows, gw), BF16) + _nbytes(w_sp.shape, BF16) + _nbytes(b_full.shape, F32)
    return pl.pallas_call(
        functools.partial(_gmlp_body, n_chunks=rows // CHUNK, n_groups=n_groups),
        grid=(t // rows,),
        in_specs=[pl.BlockSpec((rows, gw), lambda i: (i, 0)),
                  pl.BlockSpec((rows, gw), lambda i: (i, 1)),
                  pl.BlockSpec((1, gw), lambda i: (0, 0)),
                  pl.BlockSpec((1, gw), lambda i: (0, 0)),
                  pl.BlockSpec((n_groups, CHUNK, CHUNK), lambda i: (0, 0, 0)),
                  pl.BlockSpec((n_groups, CHUNK, GROUP_DIM), lambda i: (0, 0, 0))],
        out_specs=pl.BlockSpec((rows, gw), lambda i: (i, 0)),
        out_shape=jax.ShapeDtypeStruct((t, gw), BF16),
        compiler_params=pltpu.CompilerParams(
            dimension_semantics=("arbitrary",), vmem_limit_bytes=_vmem_limit(blk, 4 * rows * gw * 4)),
        name=name,
    )(uv, uv, v_norm_g.reshape(1, gw).astype(F32), v_norm_b.reshape(1, gw).astype(F32),
      w_sp.astype(BF16), b_full)


def _row_copy(src_hbm, dst_vmem, src_row, dst_row, sem):
    return pltpu.make_async_copy(src_hbm.at[pl.ds(src_row, 1)], dst_vmem.at[pl.ds(dst_row, 1)], sem)


def _moe_up_body(exp_ref, nact_ref, rows_ref, tok_ref, hp_hbm, w1_ref, w3_ref, o_ref, xbuf, xbf, sem, *, tile, gn):
    r = pl.program_id(0)
    j = pl.program_id(1)
    n_active = nact_ref[0]
    chunk = tile // gn
    half = xbuf.shape[2]

    def issue(row_tile, slot, row0, n_rows):
        base = row_tile * tile

        def f(k, c):
            row = row0 + k
            _row_copy(hp_hbm, xbuf.at[slot], tok_ref[base + row], row, sem.at[slot]).start()
            return c

        lax.fori_loop(0, n_rows, f, 0, unroll=8)

    @pl.when(jnp.logical_and(r == 0, j == 0))
    def _():
        issue(0, 0, 0, tile)

    @pl.when(jnp.logical_and(r < n_active, j == 0))
    def _():
        slot = r % 2

        def drain(k, c):
            _row_copy(hp_hbm, xbuf.at[slot], 0, k, sem.at[slot]).wait()
            return c

        lax.fori_loop(0, tile, drain, 0, unroll=8)
        hi, lo = _unpack_bf16_pairs(xbuf[slot])
        xbf[:, :half] = hi
        xbf[:, half:] = lo

    def compute(nrows):
        x = xbf[:nrows, :]
        a = jnp.dot(x, w1_ref[...].astype(BF16), preferred_element_type=F32)
        b = jnp.dot(x, w3_ref[...].astype(BF16), preferred_element_type=F32)
        o_ref[:nrows, :] = (jax.nn.silu(a) * b).astype(o_ref.dtype)
        if nrows < tile:
            o_ref[nrows:, :] = jnp.zeros((tile - nrows, o_ref.shape[1]), o_ref.dtype)

    rows = rows_ref[r]
    pl.when(rows > tile // 2)(functools.partial(compute, tile))
    pl.when(jnp.logical_and(rows > 0, rows <= tile // 2))(functools.partial(compute, tile // 2))

    @pl.when(rows == 0)
    def _():
        o_ref[...] = jnp.zeros_like(o_ref)

    @pl.when(r + 1 < n_active)
    def _():
        issue(r + 1, (r + 1) % 2, j * chunk, chunk)


def _moe_up(hp, slot_tok, group, w1, w3, lead, *, tile, tn, name):
    tile_exp, n_active, tile_rows = group
    n_slots = slot_tok.shape[0]
    half = hp.shape[1]
    d, f = w1.shape[-2], w1.shape[-1]
    gn = f // tn
    assert tile % gn == 0

    def w_map(r, j, e, na, rows, tok):
        re, je = _active_ij(r, j, na[0], gn)
        return (lead, e[re], 0, je)

    blk = 2 * _nbytes((d, tn), w1.dtype) + _nbytes((tile, tn), BF16)
    temp = (2 * _nbytes((tile, half), jnp.uint32) + _nbytes((tile, d), BF16)
            + 2 * _nbytes((d, tn), BF16) + 4 * tile * tn * 4)
    return pl.pallas_call(
        functools.partial(_moe_up_body, tile=tile, gn=gn),
        grid_spec=pltpu.PrefetchScalarGridSpec(
            num_scalar_prefetch=4, grid=(n_slots // tile, gn),
            in_specs=[pl.BlockSpec(memory_space=pl.ANY),
                      pl.BlockSpec((None, None, d, tn), w_map),
                      pl.BlockSpec((None, None, d, tn), w_map)],
            out_specs=pl.BlockSpec((tile, tn), lambda r, j, *pf: (r, j)),
            scratch_shapes=[pltpu.VMEM((2, tile, half), jnp.uint32), pltpu.VMEM((tile, d), BF16),
                            pltpu.SemaphoreType.DMA((2,))]),
        out_shape=jax.ShapeDtypeStruct((n_slots, f), BF16),
        compiler_params=pltpu.CompilerParams(
            dimension_semantics=("arbitrary", "arbitrary"), vmem_limit_bytes=_vmem_limit(blk, temp)),
        name=name,
    )(tile_exp, n_active, tile_rows, slot_tok, hp, w1, w3)


def _combine_body(slot_ref, x_ref, route_ref, g_ref, y_hbm, o_ref, buf, sem, *, rows, n_tiles, norm):
    i = pl.program_id(0)

    def issue(tile, slot):
        base = tile * rows

        def f(k, c):
            for j in range(TOP_K):
                _row_copy(y_hbm, buf.at[slot, j], slot_ref[(base + k) * TOP_K + j], k, sem.at[slot]).start()
            return c

        lax.fori_loop(0, rows, f, 0, unroll=4)

    @pl.when(i == 0)
    def _():
        issue(0, 0)

    @pl.when(i + 1 < n_tiles)
    def _():
        issue(i + 1, (i + 1) % 2)

    slot = i % 2

    def drain(k, c):
        for j in range(TOP_K):
            _row_copy(y_hbm, buf.at[slot, j], 0, k, sem.at[slot]).wait()
        return c

    lax.fori_loop(0, rows, drain, 0, unroll=4)
    route = route_ref[...]
    y = x_ref[...] + (buf[slot, 0] * route[:, 2:3] + buf[slot, 1] * route[:, 3:4])
    if norm:
        y = y * lax.rsqrt(jnp.mean(y * y, axis=-1, keepdims=True) + NORM_EPS) * g_ref[...]
    o_ref[...] = y


def _combine(x, route, slot_of_assign, yb, norm_gain, *, name):
    t, d = x.shape
    rows = _tile(t, 256, 8)
    n_tiles = t // rows
    norm = norm_gain is not None
    g = (norm_gain if norm else jnp.ones((d,), F32)).reshape(1, d).astype(F32)
    blk = 2 * _nbytes((rows, d), F32) + _nbytes((rows, V7X_LANES), F32)
    return pl.pallas_call(
        functools.partial(_combine_body, rows=rows, n_tiles=n_tiles, norm=norm),
        grid_spec=pltpu.PrefetchScalarGridSpec(
            num_scalar_prefetch=1, grid=(n_tiles,),
            in_specs=[pl.BlockSpec((rows, d), lambda i, s: (i, 0)),
                      pl.BlockSpec((rows, V7X_LANES), lambda i, s: (i, 0)),
                      pl.BlockSpec((1, d), lambda i, s: (0, 0)),
                      pl.BlockSpec(memory_space=pl.ANY)],
            out_specs=pl.BlockSpec((rows, d), lambda i, s: (i, 0)),
            scratch_shapes=[pltpu.VMEM((2, TOP_K, rows, d), F32), pltpu.SemaphoreType.DMA((2,))]),
        out_shape=jax.ShapeDtypeStruct((t, d), F32),
        compiler_params=pltpu.CompilerParams(
            dimension_semantics=("arbitrary",),
            vmem_limit_bytes=_vmem_limit(blk, (2 * TOP_K + 3) * _nbytes((rows, d), F32))),
        name=name,
    )(slot_of_assign, x, route, g, yb)


def _routing_tables(route, n_experts, tile):
    t = route.shape[0]
    a = t * TOP_K
    flat_e = route[:, :TOP_K].astype(jnp.int32).reshape(a)
    order = jnp.argsort(flat_e).astype(jnp.int32)
    rank = jnp.argsort(order).astype(jnp.int32)
    counts = jnp.sum(flat_e[:, None] == jnp.arange(n_experts, dtype=jnp.int32)[None, :], axis=0, dtype=jnp.int32)
    padded = (counts + tile - 1) // tile * tile
    pad_end = jnp.cumsum(padded)
    pad_start = pad_end - padded
    shift = pad_start - (jnp.cumsum(counts) - counts)
    slot_of_assign = rank + shift[flat_e]
    n_tiles = -(-a // tile) + n_experts
    tile_start = jnp.arange(n_tiles, dtype=jnp.int32) * tile
    tile_exp = jnp.minimum(jnp.searchsorted(pad_end, tile_start, side="right"),
                           n_experts - 1).astype(jnp.int32)
    slot = jnp.arange(n_tiles * tile, dtype=jnp.int32)
    slot_exp = jnp.repeat(tile_exp, tile)
    real = slot - pad_start[slot_exp] < counts[slot_exp]
    slot_tok = jnp.where(real, order[jnp.clip(slot - shift[slot_exp], 0, a - 1)] // TOP_K, 0)
    n_active = (pad_end[-1:] // tile).astype(jnp.int32)
    tile_rows = jnp.clip(counts[tile_exp] - (tile_start - pad_start[tile_exp]), 0, tile).astype(jnp.int32)
    return slot_tok.astype(jnp.int32), slot_of_assign.astype(jnp.int32), (tile_exp, n_active, tile_rows)


def _rope_panels(positions):
    inv_freq = ROPE_THETA ** (-jnp.arange(0, QK_ROPE, 2, dtype=F32) / QK_ROPE)
    ang = positions.astype(F32).reshape(-1, 1) * inv_freq
    cos, sin = jnp.cos(ang), jnp.sin(ang)
    z = jnp.zeros_like(cos)
    pad = jnp.zeros((cos.shape[0], V7X_LANES - QK_ROPE), F32)
    cos_t = jnp.concatenate([cos, cos, pad], axis=1)
    sin_a = jnp.concatenate([z, sin, pad], axis=1)
    sin_b = jnp.concatenate([-sin, z, pad], axis=1)
    return cos_t, sin_a, sin_b


def kernel(x, positions, mix_norm, w_in, q_norm, kv_norm, w_uq, w_ukv, v_norm_g, v_norm_b, w_sp, b_sp, w_branch_a, w_branch_b, w_out, ffn_norm, dense_w1, dense_w3, dense_w2, w_router, moe_w1, moe_w3, moe_w2, final_norm):
    batch, seq, d = x.shape
    t = batch * seq
    depth = mix_norm.shape[0]
    ql, kvl = q_norm.shape[1], kv_norm.shape[1]
    heads = w_ukv.shape[2] // (QK_NOPE + V_HEAD)
    gw = v_norm_g.shape[1]
    o2 = ql + kvl
    o3 = o2 + QK_ROPE
    o4 = o3 + 2 * gw
    assert ql % kvl == 0
    tm = _tile(t, 1024, 8)
    tn_d = _tile(d, 512)
    w_in_t = jnp.swapaxes(w_in, 1, 2)

    rope = _rope_panels(positions)
    rope_ex = [(p, "m", 0) for p in rope]
    xf = x.reshape(t, d)
    out = h = h_ss = None
    for layer in range(depth):
        w_q = jnp.pad(w_uq[layer].reshape(ql, heads, QK_NOPE + QK_ROPE),
                      ((0, 0), (0, 0), (0, HEAD_PAD - QK_NOPE - QK_ROPE))).reshape(ql, heads * HEAD_PAD).astype(BF16)
        w_kv = w_ukv[layer].astype(BF16)

        if h is None:
            h, h_ss = _rmsnorm(xf, mix_norm[layer], BF16, name=f"mix_norm{layer}"), None
        cqkv = _mm([h], [W(0, w_in_t, layer, nt=True)], _epi_id, [], o2, F32,
                   tm=tm, tn=_tile(o2, 512), rowscale=h_ss, name=f"in_latent{layer}")
        k_rope = _mm([h], [W(0, w_in_t, layer, nt=True, row0=o2)], _epi_rope_all, rope_ex, V7X_LANES, BF16,
                     tm=tm, tn=V7X_LANES, rowscale=h_ss, name=f"in_krope{layer}")
        uv = _mm([h], [W(0, w_in_t, layer, nt=True, row0=o3)], _epi_gelu, [], 2 * gw, BF16,
                 tm=tm, tn=_tile(2 * gw, 512), rowscale=h_ss, name=f"in_uv{layer}")
        gates = _mm([h], [W(0, w_in_t, layer, nt=True, row0=o4)], _epi_sigmoid, [], 2 * d, BF16,
                    tm=tm, tn=_tile(2 * d, 512), rowscale=h_ss, name=f"in_gate{layer}")
        h = None

        q = _mm([cqkv], [W(0, w_q)], _epi_q_heads, rope_ex, heads * HEAD_PAD, BF16,
                tm=tm, tn=_tile(heads * HEAD_PAD, 2048, 2 * V7X_LANES),
                prenorm=(q_norm[layer], ql, 0), name=f"q_up{layer}")
        kv = _mm([cqkv], [W(0, w_kv)], _epi_id, [], heads * (QK_NOPE + V_HEAD), BF16,
                 tm=tm, tn=_tile(heads * (QK_NOPE + V_HEAD), 2048),
                 prenorm=(kv_norm[layer], kvl, ql // kvl), name=f"kv_up{layer}")
        y_a = _attention(q, kv, k_rope, batch=batch, seq=seq, heads=heads, name=f"attention{layer}")
        y_b = _gmlp(uv, v_norm_g[layer], v_norm_b[layer], w_sp[layer], b_sp[layer], name=f"gmlp{layer}")

        merged = _mm([y_a, y_b], [W(0, w_branch_a, layer), W(1, w_branch_b, layer)],
                     _epi_gated_merge, [(gates, "mn", 0), (gates, "mn", d // tn_d)], d, BF16,
                     tm=tm, tn=tn_d, name=f"merge{layer}")
        i = layer // 2
        last = layer == depth - 1
        dense = layer % 2 == 0
        xf = _mm([merged], [W(0, w_out, layer)], _epi_residual, [(xf, "mn", 0)], d, F32,
                 tm=tm, tn=tn_d, norm_out=ffn_norm[layer] if dense else None, name=f"mix_out{layer}")
        if dense:
            ff = dense_w1.shape[2]
            xf, hg, hg_ss = xf
            act = _mm([hg], [W(0, dense_w1, i), W(0, dense_w3, i)], _epi_swiglu, [], ff, BF16,
                      tm=tm, tn=_tile(ff, 256), rowscale=hg_ss, name=f"dense_up{layer}")
            xf = _mm([act], [W(0, dense_w2[i].astype(BF16))], _epi_residual, [(xf, "mn", 0)], d, F32,
                     tm=_tile(t, 512, 8), tn=_tile(d, 512),
                     norm_out=None if last else mix_norm[layer + 1], name=f"dense_down{layer}")
            if last:
                out = _rmsnorm(xf, final_norm, F32, name="final_norm")
            else:
                xf, h, h_ss = xf
        else:
            n_experts = w_router.shape[2]
            fe = moe_w1.shape[3]
            hp, route = _norm_router(xf, ffn_norm[layer], w_router[i], name=f"ffn_norm_router{layer}")
            slot_tok, slot_of_assign, group = _routing_tables(route, n_experts, MOE_TILE)
            act = _moe_up(hp, slot_tok, group, moe_w1, moe_w3, i,
                          tile=MOE_TILE, tn=_tile(fe, 256), name=f"moe_up{layer}")
            yb = _mm([act], [W(0, moe_w2, i)], _epi_id, [], d, F32,
                     tm=MOE_TILE, tn=tn_d, group=group, name=f"moe_down{layer}")
            xf = _combine(xf, route, slot_of_assign, yb, final_norm if last else None, name=f"moe_combine{layer}")
            if last:
                out = xf
    return out.reshape(batch, seq, d)
```

```python
import functools
import math
from typing import Any, NamedTuple

import jax
import jax.numpy as jnp
from jax import lax
from jax.experimental import pallas as pl
from jax.experimental.pallas import tpu as pltpu

F32 = jnp.float32
BF16 = jnp.bfloat16

V7X_LANES = 128
V7X_VMEM_BYTES = 64 * 1024 * 1024
V7X_VMEM_CAP = V7X_VMEM_BYTES - 6 * 1024 * 1024

QK_NOPE = 128
QK_ROPE = 64
V_HEAD = 128
HEAD_PAD = 256
GROUP_DIM = 128
CHUNK = 128
TOP_K = 2
MOE_TILE = 1024
NORM_EPS = 1e-6
ROPE_THETA = 10000.0
Q_PRESCALE = float(QK_NOPE + QK_ROPE) ** -0.5 * math.log2(math.e)


def _tile(n, pref, unit=V7X_LANES):
    t = (min(pref, n) // unit) * unit
    while t >= unit:
        if n % t == 0:
            return t
        t -= unit
    return n


def _vmem_limit(block_bytes, temp_bytes):
    return int(min(V7X_VMEM_CAP, 2 * block_bytes + temp_bytes + (4 << 20)))


def _nbytes(shape, dtype):
    n = 1
    for s in shape:
        n *= s
    return n * jnp.dtype(dtype).itemsize


class W(NamedTuple):
    xi: int
    arr: Any
    lead: Any = None
    off: int = 0
    nt: bool = False
    row0: int = 0


_NT_DIMS = (((1,), (1,)), ((), ()))


def _mm_compute(x_refs, w_refs, e_refs, ss_ref, gain_ref, o_refs, ws, epi, inv_k, nrows=None):
    if nrows is not None:
        assert not e_refs and ss_ref is None and gain_ref is None
        o_ref = o_refs[0]
        x_refs = [x.at[pl.ds(0, nrows)] for x in x_refs]
        o_refs = [o_ref.at[pl.ds(0, nrows)]]
        o_ref[nrows:, :] = jnp.zeros((o_ref.shape[0] - nrows, o_ref.shape[1]), o_ref.dtype)
    accs = []
    for w, wr in zip(ws, w_refs):
        x = x_refs[w.xi][...]
        mat = wr[...].astype(BF16)
        if w.nt:
            accs.append(lax.dot_general(x, mat, _NT_DIMS, preferred_element_type=F32))
        else:
            accs.append(jnp.dot(x, mat, preferred_element_type=F32))
    if ss_ref is not None:
        r = lax.rsqrt(ss_ref[:, :1] * inv_k + NORM_EPS)
        accs = [a * r for a in accs]
    res = epi(accs, [e[...] for e in e_refs])
    o_refs[0][...] = res.astype(o_refs[0].dtype)
    if gain_ref is not None:
        xg_ref, sso_ref = o_refs[1], o_refs[2]
        xg_ref[...] = (res * gain_ref[...]).astype(xg_ref.dtype)
        part = jnp.broadcast_to(jnp.sum(res * res, axis=-1, keepdims=True), sso_ref.shape)
        j = pl.program_id(1)

        @pl.when(j == 0)
        def _():
            sso_ref[...] = part

        @pl.when(j > 0)
        def _():
            sso_ref[...] += part


def _mm_body(*refs, npf, nx, ws, ne, epi, prenorm, rowscale, norm_out, inv_k):
    pf = refs[:npf]
    refs = refs[npf:]
    x_refs, refs = refs[:nx], refs[nx:]
    w_refs, refs = refs[:len(ws)], refs[len(ws):]
    e_refs, refs = refs[:ne], refs[ne:]
    ss_ref = gain_ref = None
    if rowscale:
        ss_ref, refs = refs[0], refs[1:]
    if norm_out:
        gain_ref, refs = refs[0], refs[1:]
    n_out = 3 if norm_out else 1
    o_refs, refs = refs[:n_out], refs[n_out:]
    if prenorm:
        xn_ref = refs[0]

        @pl.when(pl.program_id(1) == 0)
        def _():
            _rmsnorm_body(x_refs[0], x_refs[1], xn_ref)

        x_refs = (xn_ref,)
    compute = functools.partial(_mm_compute, x_refs, w_refs, e_refs, ss_ref, gain_ref, o_refs, ws, epi, inv_k)
    if npf:
        rows = pf[2][pl.program_id(0)]
        half = o_refs[0].shape[0] // 2
        pl.when(rows > half)(compute)
        pl.when(jnp.logical_and(rows > 0, rows <= half))(functools.partial(compute, nrows=half))

        @pl.when(rows == 0)
        def _():
            o_refs[0][...] = jnp.zeros_like(o_refs[0])
    else:
        compute()


def _active_ij(i, j, n_active, gn):
    return jnp.minimum(i, n_active - 1), jnp.where(i < n_active, j, gn - 1)


def _mm(xs, ws, epi, extras, n, out_dtype, *, tm, tn, group=None, prenorm=None, rowscale=None, norm_out=None,
        name):
    m = xs[0].shape[0]
    gm, gn = m // tm, n // tn
    npf = 0 if group is None else 3

    def imap(f):
        if group is None:
            return f
        return lambda i, j, g, na, rows: f(*_active_ij(i, j, na[0], gn), g)

    in_specs = []
    blk = 0
    temp = (len(ws) + 2) * tm * tn * 4
    scratch = []
    if prenorm is None:
        for x in xs:
            in_specs.append(pl.BlockSpec((tm, x.shape[1]), imap(lambda i, j, *pf: (i, 0))))
            blk += _nbytes((tm, x.shape[1]), x.dtype)
    else:
        gain, width, col_block = prenorm
        assert len(xs) == 1 and group is None
        in_specs.append(pl.BlockSpec((tm, width), lambda i, j: (i, col_block)))
        in_specs.append(pl.BlockSpec((1, width), lambda i, j: (0, 0)))
        xs = [xs[0], gain.reshape(1, width).astype(F32)]
        blk += _nbytes((tm, width), F32)
        scratch.append(pltpu.VMEM((tm, width), BF16))
        temp += _nbytes((tm, width), BF16) + 2 * _nbytes((tm, width), F32)
    for w in ws:
        lead = () if w.lead is None else (w.lead,)
        none = (None,) * len(lead)
        if w.nt:
            assert group is None and w.arr.ndim == 3
            k = w.arr.shape[-1]
            in_specs.append(pl.BlockSpec((pl.Squeezed(), pl.Element(tn), pl.Element(k)), imap(
                lambda i, j, *pf, lead=w.lead, row0=w.row0: (lead, pl.multiple_of(row0 + j * tn, 8), 0))))
        elif group is None:
            assert w.arr.ndim == 2 + len(lead)
            k = w.arr.shape[-2]
            in_specs.append(pl.BlockSpec(none + (k, tn), imap(
                lambda i, j, *pf, lead=lead, off=w.off: lead + (0, j + off))))
        else:
            assert w.arr.ndim == 3 + len(lead)
            k = w.arr.shape[-2]
            in_specs.append(pl.BlockSpec(none + (None, k, tn), imap(
                lambda i, j, g, lead=lead: lead + (g[i], 0, j))))
        blk += _nbytes((k, tn), w.arr.dtype)
        if w.arr.dtype != BF16:
            temp += _nbytes((k, tn), BF16)
    for arr, kind, off in extras:
        if kind == "mn":
            in_specs.append(pl.BlockSpec((tm, tn), imap(lambda i, j, *pf, off=off: (i, j + off))))
            blk += _nbytes((tm, tn), arr.dtype)
        else:
            in_specs.append(pl.BlockSpec((tm, arr.shape[1]), imap(lambda i, j, *pf: (i, 0))))
            blk += _nbytes((tm, arr.shape[1]), arr.dtype)
    tail = []
    if rowscale is not None:
        assert group is None
        in_specs.append(pl.BlockSpec((tm, V7X_LANES), lambda i, j: (i, 0)))
        blk += _nbytes((tm, V7X_LANES), F32)
        tail.append(rowscale)
    out_specs = [pl.BlockSpec((tm, tn), lambda i, j, *pf: (i, j))]
    out_shape = [jax.ShapeDtypeStruct((m, n), out_dtype)]
    blk += _nbytes((tm, tn), out_dtype)
    if norm_out is not None:
        assert group is None
        in_specs.append(pl.BlockSpec((1, tn), lambda i, j: (0, j)))
        tail.append(norm_out.reshape(1, n).astype(F32))
        out_specs += [pl.BlockSpec((tm, tn), lambda i, j: (i, j)), pl.BlockSpec((tm, V7X_LANES), lambda i, j: (i, 0))]
        out_shape += [jax.ShapeDtypeStruct((m, n), BF16), jax.ShapeDtypeStruct((m, V7X_LANES), F32)]
        blk += _nbytes((tm, tn), BF16) + _nbytes((tm, V7X_LANES), F32)
    inv_k = 1.0 / xs[0].shape[1]
    body = functools.partial(_mm_body, npf=npf, nx=len(xs), ws=tuple(w._replace(arr=None) for w in ws),
                             ne=len(extras), epi=epi, prenorm=prenorm is not None,
                             rowscale=rowscale is not None, norm_out=norm_out is not None, inv_k=inv_k)
    call = pl.pallas_call(
        body,
        grid_spec=pltpu.PrefetchScalarGridSpec(
            num_scalar_prefetch=npf, grid=(gm, gn), in_specs=in_specs, out_specs=out_specs,
            scratch_shapes=scratch),
        out_shape=out_shape,
        compiler_params=pltpu.CompilerParams(
            dimension_semantics=("arbitrary", "arbitrary"), vmem_limit_bytes=_vmem_limit(blk, temp)),
        name=name,
    )
    args = [] if group is None else list(group)
    args += list(xs) + [w.arr for w in ws] + [a for a, _, _ in extras] + tail
    res = call(*args)
    return res if norm_out is not None else res[0]


def _epi_id(accs, ex):
    return accs[0]


def _epi_gelu(accs, ex):
    a = accs[0]
    return 0.5 * a * (1.0 + lax.erf(a * (2.0 ** -0.5)))


def _sigmoid(a):
    return 0.5 * (1.0 + jnp.tanh(0.5 * a))


def _epi_sigmoid(accs, ex):
    return _sigmoid(accs[0])


def _epi_residual(accs, ex):
    return ex[0] + accs[0]


def _epi_swiglu(accs, ex):
    return accs[0] * _sigmoid(accs[0]) * accs[1]


def _epi_gated_merge(accs, ex):
    return ex[0].astype(F32) * accs[0] + ex[1].astype(F32) * accs[1]


def _rope_lanes(t, cos_t, sin_a, sin_b):
    half = QK_ROPE // 2
    return (t * cos_t + pltpu.roll(t, half, 1) * sin_a
            + pltpu.roll(t, V7X_LANES - half, 1) * sin_b)


def _epi_rope_all(accs, ex):
    return _rope_lanes(accs[0], *ex)


def _epi_q_heads(accs, ex):
    a = accs[0]
    outs = []
    for g in range(a.shape[1] // V7X_LANES):
        t = a[:, g * V7X_LANES:(g + 1) * V7X_LANES]
        outs.append((_rope_lanes(t, *ex) if g % 2 else t) * Q_PRESCALE)
    return jnp.concatenate(outs, axis=1)


def _rmsnorm_body(x_ref, g_ref, o_ref):
    x = x_ref[...].astype(F32)
    y = x * lax.rsqrt(jnp.mean(x * x, axis=-1, keepdims=True) + NORM_EPS)
    o_ref[...] = (y * g_ref[...]).astype(o_ref.dtype)


def _rmsnorm(x, g, out_dtype, *, width=None, col_block=0, name):
    m = x.shape[0]
    width = x.shape[1] if width is None else width
    tm = _tile(m, 256, 8)
    blk = _nbytes((tm, width), x.dtype) + _nbytes((tm, width), out_dtype)
    return pl.pallas_call(
        _rmsnorm_body,
        grid=(m // tm,),
        in_specs=[pl.BlockSpec((tm, width), lambda i: (i, col_block)),
                  pl.BlockSpec((1, width), lambda i: (0, 0))],
        out_specs=pl.BlockSpec((tm, width), lambda i: (i, 0)),
        out_shape=jax.ShapeDtypeStruct((m, width), out_dtype),
        compiler_params=pltpu.CompilerParams(
            dimension_semantics=("arbitrary",), vmem_limit_bytes=_vmem_limit(blk, 3 * tm * width * 4)),
        name=name,
    )(x, g.reshape(1, width).astype(F32))


def _pack_bf16_pairs(h):
    half = h.shape[1] // 2
    bits = lax.bitcast_convert_type(h.astype(BF16).astype(F32), jnp.uint32)
    return bits[:, :half] | (bits[:, half:] >> 16)


def _unpack_bf16_pairs(p):
    hi = lax.bitcast_convert_type(p & jnp.uint32(0xFFFF0000), F32).astype(BF16)
    lo = lax.bitcast_convert_type(p << 16, F32).astype(BF16)
    return hi, lo


def _norm_router_body(x_ref, g_ref, wr_ref, h_ref, route_ref, *, n_experts):
    x = x_ref[...]
    h = x * lax.rsqrt(jnp.mean(x * x, axis=-1, keepdims=True) + NORM_EPS) * g_ref[...]
    h_ref[...] = _pack_bf16_pairs(h)
    wr = wr_ref[...]
    h_hi, w_hi = h.astype(BF16), wr.astype(BF16)
    h_lo = (h - h_hi.astype(F32)).astype(BF16)
    w_lo = (wr - w_hi.astype(F32)).astype(BF16)
    logits = (jnp.dot(h_hi, w_hi, preferred_element_type=F32)
              + (jnp.dot(h_lo, w_hi, preferred_element_type=F32) + jnp.dot(h_hi, w_lo, preferred_element_type=F32)))
    lane = lax.broadcasted_iota(jnp.int32, logits.shape, 1).astype(F32)
    neg = jnp.float32(-jnp.inf)
    far = jnp.float32(V7X_LANES)
    l1 = jnp.where(lane < n_experts, logits, neg)
    m1 = jnp.max(l1, axis=-1, keepdims=True)
    i1 = jnp.min(jnp.where(l1 == m1, lane, far), axis=-1, keepdims=True)
    l2 = jnp.where(lane == i1, neg, l1)
    m2 = jnp.max(l2, axis=-1, keepdims=True)
    i2 = jnp.min(jnp.where(l2 == m2, lane, far), axis=-1, keepdims=True)
    e = jnp.exp(m2 - m1)
    g1 = 1.0 / (1.0 + e)
    g2 = e / (1.0 + e)
    route = jnp.where(lane == 0, i1, jnp.where(lane == 1, i2, jnp.where(lane == 2, g1, jnp.where(lane == 3, g2, 0.0))))
    route_ref[...] = route


def _norm_router(x, g, w_router, *, name):
    m, d = x.shape
    n_experts = w_router.shape[1]
    wr = jnp.pad(w_router.astype(F32), ((0, 0), (0, V7X_LANES - n_experts)))
    tm = _tile(m, 256, 8)
    blk = 2 * _nbytes((tm, d), F32) + _nbytes((d, V7X_LANES), F32) + _nbytes((tm, V7X_LANES), F32)
    return pl.pallas_call(
        functools.partial(_norm_router_body, n_experts=n_experts),
        grid=(m // tm,),
        in_specs=[pl.BlockSpec((tm, d), lambda i: (i, 0)),
                  pl.BlockSpec((1, d), lambda i: (0, 0)),
                  pl.BlockSpec((d, V7X_LANES), lambda i: (0, 0))],
        out_specs=[pl.BlockSpec((tm, d // 2), lambda i: (i, 0)),
                   pl.BlockSpec((tm, V7X_LANES), lambda i: (i, 0))],
        out_shape=[jax.ShapeDtypeStruct((m, d // 2), jnp.uint32), jax.ShapeDtypeStruct((m, V7X_LANES), F32)],
        compiler_params=pltpu.CompilerParams(
            dimension_semantics=("arbitrary",), vmem_limit_bytes=_vmem_limit(blk, 4 * tm * d * 4)),
        name=name,
    )(x, g.reshape(1, d).astype(F32), wr)


def _attn_body(q_ref, kn_ref, kr_ref, v_ref, o_ref, kcat_ref, vone_ref, *, sub):
    @pl.when(pl.program_id(2) == 0)
    def _():
        kcat_ref[:, :QK_NOPE] = kn_ref[...]
        kcat_ref[:, QK_NOPE:] = kr_ref[...]
        vone_ref[:, :V_HEAD] = v_ref[...]
        vone_ref[:, V_HEAD:] = jnp.ones((v_ref.shape[0], V_HEAD), vone_ref.dtype)

    for r0 in range(0, q_ref.shape[0], sub):
        rows = slice(r0, r0 + sub)
        s = lax.dot_general(q_ref[rows, :], kcat_ref[...], (((1,), (1,)), ((), ())),
                            preferred_element_type=F32)
        p = jnp.exp2(s - jnp.max(s, axis=-1, keepdims=True))
        ol = jnp.dot(p.astype(BF16), vone_ref[...], preferred_element_type=F32)
        o_ref[rows, :] = (ol[:, :V_HEAD] / ol[:, V_HEAD:]).astype(o_ref.dtype)


def _attention(q, kv, k_rope, *, batch, seq, heads, name):
    t = batch * seq
    sub = _tile(seq, 256, 8)
    tq = _tile(seq, 8 * sub, sub)
    nq = seq // tq
    blk = (_nbytes((tq, HEAD_PAD), BF16) + 3 * _nbytes((seq, V7X_LANES), BF16) + _nbytes((tq, V_HEAD), BF16))
    return pl.pallas_call(
        functools.partial(_attn_body, sub=sub),
        grid=(batch, heads, nq),
        in_specs=[pl.BlockSpec((tq, HEAD_PAD), lambda b, h, i: (b * nq + i, h)),
                  pl.BlockSpec((seq, QK_NOPE), lambda b, h, i: (b, 2 * h)),
                  pl.BlockSpec((seq, V7X_LANES), lambda b, h, i: (b, 0)),
                  pl.BlockSpec((seq, V_HEAD), lambda b, h, i: (b, 2 * h + 1))],
        out_specs=pl.BlockSpec((tq, V_HEAD), lambda b, h, i: (b * nq + i, h)),
        out_shape=jax.ShapeDtypeStruct((t, heads * V_HEAD), BF16),
        scratch_shapes=[pltpu.VMEM((seq, HEAD_PAD), BF16), pltpu.VMEM((seq, 2 * V_HEAD), BF16)],
        compiler_params=pltpu.CompilerParams(
            dimension_semantics=("arbitrary", "arbitrary", "arbitrary"),
            vmem_limit_bytes=_vmem_limit(blk, _nbytes((seq, HEAD_PAD), BF16) + 4 * tq * seq * 4)),
        name=name,
    )(q, kv, k_rope, kv)


def _gmlp_body(u_ref, v_ref, g_ref, b_ref, w_ref, bs_ref, o_ref, *, n_chunks, n_groups):
    v = v_ref[...].astype(F32)
    mu = jnp.mean(v, axis=-1, keepdims=True)
    vc = v - mu
    vn = vc * lax.rsqrt(jnp.mean(vc * vc, axis=-1, keepdims=True) + NORM_EPS)
    vn = (vn * g_ref[...] + b_ref[...]).astype(BF16)
    for c in range(n_chunks):
        rows = slice(c * CHUNK, (c + 1) * CHUNK)
        for g in range(n_groups):
            cols = slice(g * GROUP_DIM, (g + 1) * GROUP_DIM)
            s = jnp.dot(w_ref[g], vn[rows, cols], preferred_element_type=F32) + bs_ref[g]
            o_ref[rows, cols] = (u_ref[rows, cols].astype(F32) * s).astype(o_ref.dtype)


def _gmlp(uv, v_norm_g, v_norm_b, w_sp, b_sp, *, name):
    t = uv.shape[0]
    gw = uv.shape[1] // 2
    n_groups = w_sp.shape[0]
    rows = _tile(t, 2 * CHUNK, CHUNK)
    b_full = jnp.broadcast_to(b_sp.astype(F32)[:, :, None], (n_groups, CHUNK, GROUP_DIM))
    blk = 3 * _nbytes((rows, gw), BF16) + _nbytes(w_sp.shape, BF16) + _nbytes(b_full.shape, F32)
    return pl.pallas_call(
        functools.partial(_gmlp_body, n_chunks=rows // CHUNK, n_groups=n_groups),
        grid=(t // rows,),
        in_specs=[pl.BlockSpec((rows, gw), lambda i: (i, 0)),
                  pl.BlockSpec((rows, gw), lambda i: (i, 1)),
                  pl.BlockSpec((1, gw), lambda i: (0, 0)),
                  pl.BlockSpec((1, gw), lambda i: (0, 0)),
                  pl.BlockSpec((n_groups, CHUNK, CHUNK), lambda i: (0, 0, 0)),
                  pl.BlockSpec((n_groups, CHUNK, GROUP_DIM), lambda i: (0, 0, 0))],
        out_specs=pl.BlockSpec((rows, gw), lambda i: (i, 0)),
        out_shape=jax.ShapeDtypeStruct((t, gw), BF16),
        compiler_params=pltpu.CompilerParams(
            dimension_semantics=("arbitrary",), vmem_limit_bytes=_vmem_limit(blk, 4 * rows * gw * 4)),
        name=name,
    )(uv, uv, v_norm_g.reshape(1, gw).astype(F32), v_norm_b.reshape(1, gw).astype(F32),
      w_sp.astype(BF16), b_full)


def _row_copy(src_hbm, dst_vmem, src_row, dst_row, sem):
    return pltpu.make_async_copy(src_hbm.at[pl.ds(src_row, 1)], dst_vmem.at[pl.ds(dst_row, 1)], sem)


def _moe_up_body(exp_ref, nact_ref, rows_ref, tok_ref, hp_hbm, w1_ref, w3_ref, o_ref, xbuf, xbf, sem, *, tile, gn):
    r = pl.program_id(0)
    j = pl.program_id(1)
    n_active = nact_ref[0]
    chunk = tile // gn
    half = xbuf.shape[2]

    def issue(row_tile, slot, row0, n_rows):
        base = row_tile * tile

        def f(k, c):
            row = row0 + k
            _row_copy(hp_hbm, xbuf.at[slot], tok_ref[base + row], row, sem.at[slot]).start()
            return c

        lax.fori_loop(0, n_rows, f, 0, unroll=8)

    @pl.when(jnp.logical_and(r == 0, j == 0))
    def _():
        issue(0, 0, 0, tile)

    @pl.when(jnp.logical_and(r < n_active, j == 0))
    def _():
        slot = r % 2

        def drain(k, c):
            _row_copy(hp_hbm, xbuf.at[slot], 0, k, sem.at[slot]).wait()
            return c

        lax.fori_loop(0, tile, drain, 0, unroll=8)
        hi, lo = _unpack_bf16_pairs(xbuf[slot])
        xbf[:, :half] = hi
        xbf[:, half:] = lo

    def compute(nrows):
        x = xbf[:nrows, :]
        a = jnp.dot(x, w1_ref[...].astype(BF16), preferred_element_type=F32)
        b = jnp.dot(x, w3_ref[...].astype(BF16), preferred_element_type=F32)
        o_ref[:nrows, :] = (a * _sigmoid(a) * b).astype(o_ref.dtype)
        if nrows < tile:
            o_ref[nrows:, :] = jnp.zeros((tile - nrows, o_ref.shape[1]), o_ref.dtype)

    rows = rows_ref[r]
    pl.when(rows > tile // 2)(functools.partial(compute, tile))
    pl.when(jnp.logical_and(rows > 0, rows <= tile // 2))(functools.partial(compute, tile // 2))

    @pl.when(rows == 0)
    def _():
        o_ref[...] = jnp.zeros_like(o_ref)

    @pl.when(r + 1 < n_active)
    def _():
        issue(r + 1, (r + 1) % 2, j * chunk, chunk)


def _moe_up(hp, slot_tok, group, w1, w3, lead, *, tile, tn, name):
    tile_exp, n_active, tile_rows = group
    n_slots = slot_tok.shape[0]
    half = hp.shape[1]
    d, f = w1.shape[-2], w1.shape[-1]
    gn = f // tn
    assert tile % gn == 0

    def w_map(r, j, e, na, rows, tok):
        re, je = _active_ij(r, j, na[0], gn)
        return (lead, e[re], 0, je)

    blk = 2 * _nbytes((d, tn), w1.dtype) + _nbytes((tile, tn), BF16)
    temp = (2 * _nbytes((tile, half), jnp.uint32) + _nbytes((tile, d), BF16)
            + 2 * _nbytes((d, tn), BF16) + 4 * tile * tn * 4)
    return pl.pallas_call(
        functools.partial(_moe_up_body, tile=tile, gn=gn),
        grid_spec=pltpu.PrefetchScalarGridSpec(
            num_scalar_prefetch=4, grid=(n_slots // tile, gn),
            in_specs=[pl.BlockSpec(memory_space=pl.ANY),
                      pl.BlockSpec((None, None, d, tn), w_map),
                      pl.BlockSpec((None, None, d, tn), w_map)],
            out_specs=pl.BlockSpec((tile, tn), lambda r, j, *pf: (r, j)),
            scratch_shapes=[pltpu.VMEM((2, tile, half), jnp.uint32), pltpu.VMEM((tile, d), BF16),
                            pltpu.SemaphoreType.DMA((2,))]),
        out_shape=jax.ShapeDtypeStruct((n_slots, f), BF16),
        compiler_params=pltpu.CompilerParams(
            dimension_semantics=("arbitrary", "arbitrary"), vmem_limit_bytes=_vmem_limit(blk, temp)),
        name=name,
    )(tile_exp, n_active, tile_rows, slot_tok, hp, w1, w3)


def _combine_body(slot_ref, x_ref, route_ref, g_ref, y_hbm, o_ref, buf, sem, *, rows, n_tiles, norm):
    i = pl.program_id(0)

    def issue(tile, slot):
        base = tile * rows

        def f(k, c):
            for j in range(TOP_K):
                _row_copy(y_hbm, buf.at[slot, j], slot_ref[(base + k) * TOP_K + j], k, sem.at[slot]).start()
            return c

        lax.fori_loop(0, rows, f, 0, unroll=4)

    @pl.when(i == 0)
    def _():
        issue(0, 0)

    @pl.when(i + 1 < n_tiles)
    def _():
        issue(i + 1, (i + 1) % 2)

    slot = i % 2

    def drain(k, c):
        for j in range(TOP_K):
            _row_copy(y_hbm, buf.at[slot, j], 0, k, sem.at[slot]).wait()
        return c

    lax.fori_loop(0, rows, drain, 0, unroll=4)
    route = route_ref[...]
    y = x_ref[...] + (buf[slot, 0] * route[:, 2:3] + buf[slot, 1] * route[:, 3:4])
    if norm:
        y = y * lax.rsqrt(jnp.mean(y * y, axis=-1, keepdims=True) + NORM_EPS) * g_ref[...]
    o_ref[...] = y


def _combine(x, route, slot_of_assign, yb, norm_gain, *, name):
    t, d = x.shape
    rows = _tile(t, 256, 8)
    n_tiles = t // rows
    norm = norm_gain is not None
    g = (norm_gain if norm else jnp.ones((d,), F32)).reshape(1, d).astype(F32)
    blk = 2 * _nbytes((rows, d), F32) + _nbytes((rows, V7X_LANES), F32)
    return pl.pallas_call(
        functools.partial(_combine_body, rows=rows, n_tiles=n_tiles, norm=norm),
        grid_spec=pltpu.PrefetchScalarGridSpec(
            num_scalar_prefetch=1, grid=(n_tiles,),
            in_specs=[pl.BlockSpec((rows, d), lambda i, s: (i, 0)),
                      pl.BlockSpec((rows, V7X_LANES), lambda i, s: (i, 0)),
                      pl.BlockSpec((1, d), lambda i, s: (0, 0)),
                      pl.BlockSpec(memory_space=pl.ANY)],
            out_specs=pl.BlockSpec((rows, d), lambda i, s: (i, 0)),
            scratch_shapes=[pltpu.VMEM((2, TOP_K, rows, d), F32), pltpu.SemaphoreType.DMA((2,))]),
        out_shape=jax.ShapeDtypeStruct((t, d), F32),
        compiler_params=pltpu.CompilerParams(
            dimension_semantics=("arbitrary",),
            vmem_limit_bytes=_vmem_limit(blk, (2 * TOP_K + 3) * _nbytes((rows, d), F32))),
        name=name,
    )(slot_of_assign, x, route, g, yb)


def _routing_tables(route, n_experts, tile):
    t = route.shape[0]
    a = t * TOP_K
    flat_e = route[:, :TOP_K].astype(jnp.int32).reshape(a)
    order = jnp.argsort(flat_e).astype(jnp.int32)
    rank = jnp.argsort(order).astype(jnp.int32)
    counts = jnp.sum(flat_e[:, None] == jnp.arange(n_experts, dtype=jnp.int32)[None, :], axis=0, dtype=jnp.int32)
    padded = (counts + tile - 1) // tile * tile
    pad_end = jnp.cumsum(padded)
    pad_start = pad_end - padded
    shift = pad_start - (jnp.cumsum(counts) - counts)
    slot_of_assign = rank + shift[flat_e]
    n_tiles = -(-a // tile) + n_experts
    tile_start = jnp.arange(n_tiles, dtype=jnp.int32) * tile
    tile_exp = jnp.minimum(jnp.searchsorted(pad_end, tile_start, side="right"),
                           n_experts - 1).astype(jnp.int32)
    slot = jnp.arange(n_tiles * tile, dtype=jnp.int32)
    slot_exp = jnp.repeat(tile_exp, tile)
    real = slot - pad_start[slot_exp] < counts[slot_exp]
    slot_tok = jnp.where(real, order[jnp.clip(slot - shift[slot_exp], 0, a - 1)] // TOP_K, 0)
    n_active = (pad_end[-1:] // tile).astype(jnp.int32)
    tile_rows = jnp.clip(counts[tile_exp] - (tile_start - pad_start[tile_exp]), 0, tile).astype(jnp.int32)
    return slot_tok.astype(jnp.int32), slot_of_assign.astype(jnp.int32), (tile_exp, n_active, tile_rows)


def _rope_panels(positions):
    inv_freq = ROPE_THETA ** (-jnp.arange(0, QK_ROPE, 2, dtype=F32) / QK_ROPE)
    ang = positions.astype(F32).reshape(-1, 1) * inv_freq
    cos, sin = jnp.cos(ang), jnp.sin(ang)
    z = jnp.zeros_like(cos)
    pad = jnp.zeros((cos.shape[0], V7X_LANES - QK_ROPE), F32)
    cos_t = jnp.concatenate([cos, cos, pad], axis=1)
    sin_a = jnp.concatenate([z, sin, pad], axis=1)
    sin_b = jnp.concatenate([-sin, z, pad], axis=1)
    return cos_t, sin_a, sin_b


def kernel(x, positions, mix_norm, w_in, q_norm, kv_norm, w_uq, w_ukv, v_norm_g, v_norm_b, w_sp, b_sp, w_branch_a, w_branch_b, w_out, ffn_norm, dense_w1, dense_w3, dense_w2, w_router, moe_w1, moe_w3, moe_w2, final_norm):
    batch, seq, d = x.shape
    t = batch * seq
    depth = mix_norm.shape[0]
    ql, kvl = q_norm.shape[1], kv_norm.shape[1]
    heads = w_ukv.shape[2] // (QK_NOPE + V_HEAD)
    gw = v_norm_g.shape[1]
    o2 = ql + kvl
    o3 = o2 + QK_ROPE
    o4 = o3 + 2 * gw
    assert ql % kvl == 0
    tm = _tile(t, 1024, 8)
    tn_d = _tile(d, 512)
    w_in_t = jnp.swapaxes(w_in, 1, 2)

    rope = _rope_panels(positions)
    rope_ex = [(p, "m", 0) for p in rope]
    xf = x.reshape(t, d)
    out = h = h_ss = None
    for layer in range(depth):
        w_q = jnp.pad(w_uq[layer].reshape(ql, heads, QK_NOPE + QK_ROPE),
                      ((0, 0), (0, 0), (0, HEAD_PAD - QK_NOPE - QK_ROPE))).reshape(ql, heads * HEAD_PAD).astype(BF16)
        w_kv = w_ukv[layer].astype(BF16)

        if h is None:
            h, h_ss = _rmsnorm(xf, mix_norm[layer], BF16, name=f"mix_norm{layer}"), None
        cqkv = _mm([h], [W(0, w_in_t, layer, nt=True)], _epi_id, [], o2, F32,
                   tm=tm, tn=_tile(o2, 512), rowscale=h_ss, name=f"in_latent{layer}")
        k_rope = _mm([h], [W(0, w_in_t, layer, nt=True, row0=o2)], _epi_rope_all, rope_ex, V7X_LANES, BF16,
                     tm=tm, tn=V7X_LANES, rowscale=h_ss, name=f"in_krope{layer}")
        uv = _mm([h], [W(0, w_in_t, layer, nt=True, row0=o3)], _epi_gelu, [], 2 * gw, BF16,
                 tm=tm, tn=_tile(2 * gw, 512), rowscale=h_ss, name=f"in_uv{layer}")
        gates = _mm([h], [W(0, w_in_t, layer, nt=True, row0=o4)], _epi_sigmoid, [], 2 * d, BF16,
                    tm=tm, tn=_tile(2 * d, 512), rowscale=h_ss, name=f"in_gate{layer}")
        h = None

        q = _mm([cqkv], [W(0, w_q)], _epi_q_heads, rope_ex, heads * HEAD_PAD, BF16,
                tm=tm, tn=_tile(heads * HEAD_PAD, 2048, 2 * V7X_LANES),
                prenorm=(q_norm[layer], ql, 0), name=f"q_up{layer}")
        kv = _mm([cqkv], [W(0, w_kv)], _epi_id, [], heads * (QK_NOPE + V_HEAD), BF16,
                 tm=tm, tn=_tile(heads * (QK_NOPE + V_HEAD), 2048),
                 prenorm=(kv_norm[layer], kvl, ql // kvl), name=f"kv_up{layer}")
        y_a = _attention(q, kv, k_rope, batch=batch, seq=seq, heads=heads, name=f"attention{layer}")
        y_b = _gmlp(uv, v_norm_g[layer], v_norm_b[layer], w_sp[layer], b_sp[layer], name=f"gmlp{layer}")

        merged = _mm([y_a, y_b], [W(0, w_branch_a, layer), W(1, w_branch_b, layer)],
                     _epi_gated_merge, [(gates, "mn", 0), (gates, "mn", d // tn_d)], d, BF16,
                     tm=tm, tn=tn_d, name=f"merge{layer}")
        i = layer // 2
        last = layer == depth - 1
        dense = layer % 2 == 0
        xf = _mm([merged], [W(0, w_out, layer)], _epi_residual, [(xf, "mn", 0)], d, F32,
                 tm=tm, tn=tn_d, norm_out=ffn_norm[layer] if dense else None, name=f"mix_out{layer}")
        if dense:
            ff = dense_w1.shape[2]
            xf, hg, hg_ss = xf
            act = _mm([hg], [W(0, dense_w1, i), W(0, dense_w3, i)], _epi_swiglu, [], ff, BF16,
                      tm=tm, tn=_tile(ff, 256), rowscale=hg_ss, name=f"dense_up{layer}")
            xf = _mm([act], [W(0, dense_w2[i].astype(BF16))], _epi_residual, [(xf, "mn", 0)], d, F32,
                     tm=_tile(t, 512, 8), tn=_tile(d, 512),
                     norm_out=None if last else mix_norm[layer + 1], name=f"dense_down{layer}")
            if last:
                out = _rmsnorm(xf, final_norm, F32, name="final_norm")
            else:
                xf, h, h_ss = xf
        else:
            n_experts = w_router.shape[2]
            fe = moe_w1.shape[3]
            hp, route = _norm_router(xf, ffn_norm[layer], w_router[i], name=f"ffn_norm_router{layer}")
            slot_tok, slot_of_assign, group = _routing_tables(route, n_experts, MOE_TILE)
            act = _moe_up(hp, slot_tok, group, moe_w1, moe_w3, i,
                          tile=MOE_TILE, tn=_tile(fe, 256), name=f"moe_up{layer}")
            yb = _mm([act], [W(0, moe_w2, i)], _epi_id, [], d, F32,
                     tm=MOE_TILE, tn=tn_d, group=group, name=f"moe_down{layer}")
            xf = _combine(xf, route, slot_of_assign, yb, final_norm if last else None, name=f"moe_combine{layer}")
            if last:
                out = xf
    return out.reshape(batch, seq, d)
```

```python
import functools
import math
from typing import Any, NamedTuple

import jax
import jax.numpy as jnp
from jax import lax
from jax.experimental import pallas as pl
from jax.experimental.pallas import tpu as pltpu

F32 = jnp.float32
BF16 = jnp.bfloat16

V7X_LANES = 128
V7X_VMEM_BYTES = 64 * 1024 * 1024
V7X_VMEM_CAP = V7X_VMEM_BYTES - 6 * 1024 * 1024

QK_NOPE = 128
QK_ROPE = 64
V_HEAD = 128
HEAD_PAD = 256
GROUP_DIM = 128
CHUNK = 128
TOP_K = 2
MOE_TILE = 1024
NORM_EPS = 1e-6
ROPE_THETA = 10000.0
Q_PRESCALE = float(QK_NOPE + QK_ROPE) ** -0.5 * math.log2(math.e)


def _tile(n, pref, unit=V7X_LANES):
    t = (min(pref, n) // unit) * unit
    while t >= unit:
        if n % t == 0:
            return t
        t -= unit
    return n


def _vmem_limit(block_bytes, temp_bytes):
    return int(min(V7X_VMEM_CAP, 2 * block_bytes + temp_bytes + (4 << 20)))


def _nbytes(shape, dtype):
    n = 1
    for s in shape:
        n *= s
    return n * jnp.dtype(dtype).itemsize


class W(NamedTuple):
    xi: int
    arr: Any
    lead: Any = None
    off: int = 0
    nt: bool = False
    row0: int = 0


_NT_DIMS = (((1,), (1,)), ((), ()))


def _mm_compute(x_refs, w_refs, e_refs, ss_ref, gain_ref, o_refs, ws, epi, inv_k, nrows=None):
    if nrows is not None:
        assert not e_refs and ss_ref is None and gain_ref is None
        o_ref = o_refs[0]
        x_refs = [x.at[pl.ds(0, nrows)] for x in x_refs]
        o_refs = [o_ref.at[pl.ds(0, nrows)]]
        o_ref[nrows:, :] = jnp.zeros((o_ref.shape[0] - nrows, o_ref.shape[1]), o_ref.dtype)
    accs = []
    for w, wr in zip(ws, w_refs):
        x = x_refs[w.xi][...]
        mat = wr[...].astype(BF16)
        if w.nt:
            accs.append(lax.dot_general(x, mat, _NT_DIMS, preferred_element_type=F32))
        else:
            accs.append(jnp.dot(x, mat, preferred_element_type=F32))
    if ss_ref is not None:
        r = lax.rsqrt(ss_ref[:, :1] * inv_k + NORM_EPS)
        accs = [a * r for a in accs]
    res = epi(accs, [e[...] for e in e_refs])
    o_refs[0][...] = res.astype(o_refs[0].dtype)
    if gain_ref is not None:
        xg_ref, sso_ref = o_refs[1], o_refs[2]
        xg_ref[...] = (res * gain_ref[...]).astype(xg_ref.dtype)
        part = jnp.broadcast_to(jnp.sum(res * res, axis=-1, keepdims=True), sso_ref.shape)
        j = pl.program_id(1)

        @pl.when(j == 0)
        def _():
            sso_ref[...] = part

        @pl.when(j > 0)
        def _():
            sso_ref[...] += part


def _mm_body(*refs, npf, nx, ws, ne, epi, prenorm, rowscale, norm_out, sidecast, inv_k):
    pf = refs[:npf]
    refs = refs[npf:]
    x_refs, refs = refs[:nx], refs[nx:]
    w_refs, refs = refs[:len(ws)], refs[len(ws):]
    e_refs, refs = refs[:ne], refs[ne:]
    ss_ref = gain_ref = None
    if rowscale:
        ss_ref, refs = refs[0], refs[1:]
    if norm_out:
        gain_ref, refs = refs[0], refs[1:]
    if sidecast:
        side_in, refs = refs[0], refs[1:]
    n_out = 3 if norm_out else 1
    o_refs, refs = refs[:n_out], refs[n_out:]
    if sidecast:
        side_out, refs = refs[0], refs[1:]
        side_out[...] = side_in[...].astype(side_out.dtype)
    if prenorm:
        xn_ref = refs[0]

        @pl.when(pl.program_id(1) == 0)
        def _():
            _rmsnorm_body(x_refs[0], x_refs[1], xn_ref)

        x_refs = (xn_ref,)
    compute = functools.partial(_mm_compute, x_refs, w_refs, e_refs, ss_ref, gain_ref, o_refs, ws, epi, inv_k)
    if npf:
        rows = pf[2][pl.program_id(0)]
        half = o_refs[0].shape[0] // 2
        pl.when(rows > half)(compute)
        pl.when(jnp.logical_and(rows > 0, rows <= half))(functools.partial(compute, nrows=half))

        @pl.when(rows == 0)
        def _():
            o_refs[0][...] = jnp.zeros_like(o_refs[0])
    else:
        compute()


def _active_ij(i, j, n_active, gn):
    return jnp.minimum(i, n_active - 1), jnp.where(i < n_active, j, gn - 1)


def _mm(xs, ws, epi, extras, n, out_dtype, *, tm, tn, group=None, prenorm=None, rowscale=None, norm_out=None,
        sidecast=None, name):
    m = xs[0].shape[0]
    gm, gn = m // tm, n // tn
    npf = 0 if group is None else 3

    def imap(f):
        if group is None:
            return f
        return lambda i, j, g, na, rows: f(*_active_ij(i, j, na[0], gn), g)

    in_specs = []
    blk = 0
    temp = (len(ws) + 2) * tm * tn * 4
    scratch = []
    if prenorm is None:
        for x in xs:
            in_specs.append(pl.BlockSpec((tm, x.shape[1]), imap(lambda i, j, *pf: (i, 0))))
            blk += _nbytes((tm, x.shape[1]), x.dtype)
    else:
        gain, width, col_block = prenorm
        assert len(xs) == 1 and group is None
        in_specs.append(pl.BlockSpec((tm, width), lambda i, j: (i, col_block)))
        in_specs.append(pl.BlockSpec((1, width), lambda i, j: (0, 0)))
        xs = [xs[0], gain.reshape(1, width).astype(F32)]
        blk += _nbytes((tm, width), F32)
        scratch.append(pltpu.VMEM((tm, width), BF16))
        temp += _nbytes((tm, width), BF16) + 2 * _nbytes((tm, width), F32)
    for w in ws:
        lead = () if w.lead is None else (w.lead,)
        none = (None,) * len(lead)
        if w.nt:
            assert group is None and w.arr.ndim == 3
            k = w.arr.shape[-1]
            in_specs.append(pl.BlockSpec((pl.Squeezed(), pl.Element(tn), pl.Element(k)), imap(
                lambda i, j, *pf, lead=w.lead, row0=w.row0: (lead, pl.multiple_of(row0 + j * tn, 8), 0))))
        elif group is None:
            assert w.arr.ndim == 2 + len(lead)
            k = w.arr.shape[-2]
            in_specs.append(pl.BlockSpec(none + (k, tn), imap(
                lambda i, j, *pf, lead=lead, off=w.off: lead + (0, j + off))))
        else:
            assert w.arr.ndim == 3 + len(lead)
            k = w.arr.shape[-2]
            in_specs.append(pl.BlockSpec(none + (None, k, tn), imap(
                lambda i, j, g, lead=lead: lead + (g[i], 0, j))))
        blk += _nbytes((k, tn), w.arr.dtype)
        if w.arr.dtype != BF16:
            temp += _nbytes((k, tn), BF16)
    for arr, kind, off in extras:
        if kind == "mn":
            in_specs.append(pl.BlockSpec((tm, tn), imap(lambda i, j, *pf, off=off: (i, j + off))))
            blk += _nbytes((tm, tn), arr.dtype)
        else:
            in_specs.append(pl.BlockSpec((tm, arr.shape[1]), imap(lambda i, j, *pf: (i, 0))))
            blk += _nbytes((tm, arr.shape[1]), arr.dtype)
    tail = []
    if rowscale is not None:
        assert group is None
        in_specs.append(pl.BlockSpec((tm, V7X_LANES), lambda i, j: (i, 0)))
        blk += _nbytes((tm, V7X_LANES), F32)
        tail.append(rowscale)
    out_specs = [pl.BlockSpec((tm, tn), lambda i, j, *pf: (i, j))]
    out_shape = [jax.ShapeDtypeStruct((m, n), out_dtype)]
    blk += _nbytes((tm, tn), out_dtype)
    if norm_out is not None:
        assert group is None
        in_specs.append(pl.BlockSpec((1, tn), lambda i, j: (0, j)))
        tail.append(norm_out.reshape(1, n).astype(F32))
        out_specs += [pl.BlockSpec((tm, tn), lambda i, j: (i, j)), pl.BlockSpec((tm, V7X_LANES), lambda i, j: (i, 0))]
        out_shape += [jax.ShapeDtypeStruct((m, n), BF16), jax.ShapeDtypeStruct((m, V7X_LANES), F32)]
        blk += _nbytes((tm, tn), BF16) + _nbytes((tm, V7X_LANES), F32)
    if sidecast is not None:
        assert group is None and sidecast.shape[0] % (gm * gn) == 0
        side_blk = (sidecast.shape[0] // (gm * gn), sidecast.shape[1])
        in_specs.append(pl.BlockSpec(side_blk, lambda i, j: (i * gn + j, 0)))
        tail.append(sidecast)
        out_specs.append(pl.BlockSpec(side_blk, lambda i, j: (i * gn + j, 0)))
        out_shape.append(jax.ShapeDtypeStruct(sidecast.shape, BF16))
        blk += _nbytes(side_blk, sidecast.dtype) + _nbytes(side_blk, BF16)
    inv_k = 1.0 / xs[0].shape[1]
    body = functools.partial(_mm_body, npf=npf, nx=len(xs), ws=tuple(w._replace(arr=None) for w in ws),
                             ne=len(extras), epi=epi, prenorm=prenorm is not None,
                             rowscale=rowscale is not None, norm_out=norm_out is not None,
                             sidecast=sidecast is not None, inv_k=inv_k)
    call = pl.pallas_call(
        body,
        grid_spec=pltpu.PrefetchScalarGridSpec(
            num_scalar_prefetch=npf, grid=(gm, gn), in_specs=in_specs, out_specs=out_specs,
            scratch_shapes=scratch),
        out_shape=out_shape,
        compiler_params=pltpu.CompilerParams(
            dimension_semantics=("arbitrary", "arbitrary"), vmem_limit_bytes=_vmem_limit(blk, temp)),
        name=name,
    )
    args = [] if group is None else list(group)
    args += list(xs) + [w.arr for w in ws] + [a for a, _, _ in extras] + tail
    res = call(*args)
    return res if len(res) > 1 else res[0]


def _epi_id(accs, ex):
    return accs[0]


def _epi_gelu(accs, ex):
    a = accs[0]
    return 0.5 * a * (1.0 + lax.erf(a * (2.0 ** -0.5)))


def _sigmoid(a):
    return 0.5 * (1.0 + jnp.tanh(0.5 * a))


def _epi_sigmoid(accs, ex):
    return _sigmoid(accs[0])


def _epi_residual(accs, ex):
    return ex[0] + accs[0]


def _epi_swiglu(accs, ex):
    return accs[0] * _sigmoid(accs[0]) * accs[1]


def _epi_gated_merge(accs, ex):
    return ex[0].astype(F32) * accs[0] + ex[1].astype(F32) * accs[1]


def _rope_lanes(t, cos_t, sin_a, sin_b):
    half = QK_ROPE // 2
    return (t * cos_t + pltpu.roll(t, half, 1) * sin_a
            + pltpu.roll(t, V7X_LANES - half, 1) * sin_b)


def _epi_rope_all(accs, ex):
    return _rope_lanes(accs[0], *ex)


def _epi_q_heads(accs, ex):
    a = accs[0]
    outs = []
    for g in range(a.shape[1] // V7X_LANES):
        t = a[:, g * V7X_LANES:(g + 1) * V7X_LANES]
        outs.append((_rope_lanes(t, *ex) if g % 2 else t) * Q_PRESCALE)
    return jnp.concatenate(outs, axis=1)


def _rmsnorm_body(x_ref, g_ref, o_ref):
    x = x_ref[...].astype(F32)
    y = x * lax.rsqrt(jnp.mean(x * x, axis=-1, keepdims=True) + NORM_EPS)
    o_ref[...] = (y * g_ref[...]).astype(o_ref.dtype)


def _rmsnorm(x, g, out_dtype, *, width=None, col_block=0, name):
    m = x.shape[0]
    width = x.shape[1] if width is None else width
    tm = _tile(m, 256, 8)
    blk = _nbytes((tm, width), x.dtype) + _nbytes((tm, width), out_dtype)
    return pl.pallas_call(
        _rmsnorm_body,
        grid=(m // tm,),
        in_specs=[pl.BlockSpec((tm, width), lambda i: (i, col_block)),
                  pl.BlockSpec((1, width), lambda i: (0, 0))],
        out_specs=pl.BlockSpec((tm, width), lambda i: (i, 0)),
        out_shape=jax.ShapeDtypeStruct((m, width), out_dtype),
        compiler_params=pltpu.CompilerParams(
            dimension_semantics=("arbitrary",), vmem_limit_bytes=_vmem_limit(blk, 3 * tm * width * 4)),
        name=name,
    )(x, g.reshape(1, width).astype(F32))


def _pack_bf16_pairs(h):
    half = h.shape[1] // 2
    bits = lax.bitcast_convert_type(h.astype(BF16).astype(F32), jnp.uint32)
    return bits[:, :half] | (bits[:, half:] >> 16)


def _unpack_bf16_pairs(p):
    hi = lax.bitcast_convert_type(p & jnp.uint32(0xFFFF0000), F32).astype(BF16)
    lo = lax.bitcast_convert_type(p << 16, F32).astype(BF16)
    return hi, lo


def _norm_router_body(x_ref, g_ref, wr_ref, h_ref, route_ref, *, n_experts):
    x = x_ref[...]
    h = x * lax.rsqrt(jnp.mean(x * x, axis=-1, keepdims=True) + NORM_EPS) * g_ref[...]
    h_ref[...] = _pack_bf16_pairs(h)
    wr = wr_ref[...]
    h_hi, w_hi = h.astype(BF16), wr.astype(BF16)
    h_lo = (h - h_hi.astype(F32)).astype(BF16)
    w_lo = (wr - w_hi.astype(F32)).astype(BF16)
    logits = (jnp.dot(h_hi, w_hi, preferred_element_type=F32)
              + (jnp.dot(h_lo, w_hi, preferred_element_type=F32) + jnp.dot(h_hi, w_lo, preferred_element_type=F32)))
    lane = lax.broadcasted_iota(jnp.int32, logits.shape, 1).astype(F32)
    neg = jnp.float32(-jnp.inf)
    far = jnp.float32(V7X_LANES)
    l1 = jnp.where(lane < n_experts, logits, neg)
    m1 = jnp.max(l1, axis=-1, keepdims=True)
    i1 = jnp.min(jnp.where(l1 == m1, lane, far), axis=-1, keepdims=True)
    l2 = jnp.where(lane == i1, neg, l1)
    m2 = jnp.max(l2, axis=-1, keepdims=True)
    i2 = jnp.min(jnp.where(l2 == m2, lane, far), axis=-1, keepdims=True)
    e = jnp.exp(m2 - m1)
    g1 = 1.0 / (1.0 + e)
    g2 = e / (1.0 + e)
    route = jnp.where(lane == 0, i1, jnp.where(lane == 1, i2, jnp.where(lane == 2, g1, jnp.where(lane == 3, g2, 0.0))))
    route_ref[...] = route


def _norm_router(x, g, w_router, *, name):
    m, d = x.shape
    n_experts = w_router.shape[1]
    wr = jnp.pad(w_router.astype(F32), ((0, 0), (0, V7X_LANES - n_experts)))
    tm = _tile(m, 256, 8)
    blk = 2 * _nbytes((tm, d), F32) + _nbytes((d, V7X_LANES), F32) + _nbytes((tm, V7X_LANES), F32)
    return pl.pallas_call(
        functools.partial(_norm_router_body, n_experts=n_experts),
        grid=(m // tm,),
        in_specs=[pl.BlockSpec((tm, d), lambda i: (i, 0)),
                  pl.BlockSpec((1, d), lambda i: (0, 0)),
                  pl.BlockSpec((d, V7X_LANES), lambda i: (0, 0))],
        out_specs=[pl.BlockSpec((tm, d // 2), lambda i: (i, 0)),
                   pl.BlockSpec((tm, V7X_LANES), lambda i: (i, 0))],
        out_shape=[jax.ShapeDtypeStruct((m, d // 2), jnp.uint32), jax.ShapeDtypeStruct((m, V7X_LANES), F32)],
        compiler_params=pltpu.CompilerParams(
            dimension_semantics=("arbitrary",), vmem_limit_bytes=_vmem_limit(blk, 4 * tm * d * 4)),
        name=name,
    )(x, g.reshape(1, d).astype(F32), wr)


def _attn_body(q_ref, kn_ref, kr_ref, v_ref, o_ref, kcat_ref, vone_ref, *, sub):
    @pl.when(pl.program_id(2) == 0)
    def _():
        kcat_ref[:, :QK_NOPE] = kn_ref[...]
        kcat_ref[:, QK_NOPE:] = kr_ref[...]
        vone_ref[:, :V_HEAD] = v_ref[...]
        vone_ref[:, V_HEAD:] = jnp.ones((v_ref.shape[0], V_HEAD), vone_ref.dtype)

    for r0 in range(0, q_ref.shape[0], sub):
        rows = slice(r0, r0 + sub)
        s = lax.dot_general(q_ref[rows, :], kcat_ref[...], (((1,), (1,)), ((), ())),
                            preferred_element_type=F32)
        p = jnp.exp2(s - jnp.max(s, axis=-1, keepdims=True))
        ol = jnp.dot(p.astype(BF16), vone_ref[...], preferred_element_type=F32)
        o_ref[rows, :] = (ol[:, :V_HEAD] / ol[:, V_HEAD:]).astype(o_ref.dtype)


def _attention(q, kv, k_rope, *, batch, seq, heads, name):
    t = batch * seq
    sub = _tile(seq, 256, 8)
    tq = _tile(seq, 8 * sub, sub)
    nq = seq // tq
    blk = (_nbytes((tq, HEAD_PAD), BF16) + 3 * _nbytes((seq, V7X_LANES), BF16) + _nbytes((tq, V_HEAD), BF16))
    return pl.pallas_call(
        functools.partial(_attn_body, sub=sub),
        grid=(batch, heads, nq),
        in_specs=[pl.BlockSpec((tq, HEAD_PAD), lambda b, h, i: (b * nq + i, h)),
                  pl.BlockSpec((seq, QK_NOPE), lambda b, h, i: (b, 2 * h)),
                  pl.BlockSpec((seq, V7X_LANES), lambda b, h, i: (b, 0)),
                  pl.BlockSpec((seq, V_HEAD), lambda b, h, i: (b, 2 * h + 1))],
        out_specs=pl.BlockSpec((tq, V_HEAD), lambda b, h, i: (b * nq + i, h)),
        out_shape=jax.ShapeDtypeStruct((t, heads * V_HEAD), BF16),
        scratch_shapes=[pltpu.VMEM((seq, HEAD_PAD), BF16), pltpu.VMEM((seq, 2 * V_HEAD), BF16)],
        compiler_params=pltpu.CompilerParams(
            dimension_semantics=("arbitrary", "arbitrary", "arbitrary"),
            vmem_limit_bytes=_vmem_limit(blk, _nbytes((seq, HEAD_PAD), BF16) + 4 * tq * seq * 4)),
        name=name,
    )(q, kv, k_rope, kv)


def _gmlp_body(u_ref, v_ref, g_ref, b_ref, w_ref, bs_ref, o_ref, *, n_chunks, n_groups):
    v = v_ref[...].astype(F32)
    mu = jnp.mean(v, axis=-1, keepdims=True)
    vc = v - mu
    vn = vc * lax.rsqrt(jnp.mean(vc * vc, axis=-1, keepdims=True) + NORM_EPS)
    vn = (vn * g_ref[...] + b_ref[...]).astype(BF16)
    for c in range(n_chunks):
        rows = slice(c * CHUNK, (c + 1) * CHUNK)
        for g in range(n_groups):
            cols = slice(g * GROUP_DIM, (g + 1) * GROUP_DIM)
            s = jnp.dot(w_ref[g], vn[rows, cols], preferred_element_type=F32) + bs_ref[g]
            o_ref[rows, cols] = (u_ref[rows, cols].astype(F32) * s).astype(o_ref.dtype)


def _gmlp(uv, v_norm_g, v_norm_b, w_sp, b_sp, *, name):
    t = uv.shape[0]
    gw = uv.shape[1] // 2
    n_groups = w_sp.shape[0]
    rows = _tile(t, 2 * CHUNK, CHUNK)
    b_full = jnp.broadcast_to(b_sp.astype(F32)[:, :, None], (n_groups, CHUNK, GROUP_DIM))
    blk = 3 * _nbytes((rows, gw), BF16) + _nbytes(w_sp.shape, BF16) + _nbytes(b_full.shape, F32)
    return pl.pallas_call(
        functools.partial(_gmlp_body, n_chunks=rows // CHUNK, n_groups=n_groups),
        grid=(t // rows,),
        in_specs=[pl.BlockSpec((rows, gw), lambda i: (i, 0)),
                  pl.BlockSpec((rows, gw), lambda i: (i, 1)),
                  pl.BlockSpec((1, gw), lambda i: (0, 0)),
                  pl.BlockSpec((1, gw), lambda i: (0, 0)),
                  pl.BlockSpec((n_groups, CHUNK, CHUNK), lambda i: (0, 0, 0)),
                  pl.BlockSpec((n_groups, CHUNK, GROUP_DIM), lambda i: (0, 0, 0))],
        out_specs=pl.BlockSpec((rows, gw), lambda i: (i, 0)),
        out_shape=jax.ShapeDtypeStruct((t, gw), BF16),
        compiler_params=pltpu.CompilerParams(
            dimension_semantics=("arbitrary",), vmem_limit_bytes=_vmem_limit(blk, 4 * rows * gw * 4)),
        name=name,
    )(uv, uv, v_norm_g.reshape(1, gw).astype(F32), v_norm_b.reshape(1, gw).astype(F32),
      w_sp.astype(BF16), b_full)


def _row_copy(src_hbm, dst_vmem, src_row, dst_row, sem):
    return pltpu.make_async_copy(src_hbm.at[pl.ds(src_row, 1)], dst_vmem.at[pl.ds(dst_row, 1)], sem)


def _moe_up_body(exp_ref, nact_ref, rows_ref, tok_ref, hp_hbm, w1_ref, w3_ref, o_ref, xbuf, xbf, sem, *, tile, gn):
    r = pl.program_id(0)
    j = pl.program_id(1)
    n_active = nact_ref[0]
    chunk = tile // gn
    half = xbuf.shape[2]

    def issue(row_tile, slot, row0, n_rows):
        base = row_tile * tile

        def f(k, c):
            row = row0 + k
            _row_copy(hp_hbm, xbuf.at[slot], tok_ref[base + row], row, sem.at[slot]).start()
            return c

        lax.fori_loop(0, n_rows, f, 0, unroll=8)

    @pl.when(jnp.logical_and(r == 0, j == 0))
    def _():
        issue(0, 0, 0, tile)

    @pl.when(jnp.logical_and(r < n_active, j == 0))
    def _():
        slot = r % 2

        def drain(k, c):
            _row_copy(hp_hbm, xbuf.at[slot], 0, k, sem.at[slot]).wait()
            return c

        lax.fori_loop(0, tile, drain, 0, unroll=8)
        hi, lo = _unpack_bf16_pairs(xbuf[slot])
        xbf[:, :half] = hi
        xbf[:, half:] = lo

    def compute(nrows):
        x = xbf[:nrows, :]
        a = jnp.dot(x, w1_ref[...].astype(BF16), preferred_element_type=F32)
        b = jnp.dot(x, w3_ref[...].astype(BF16), preferred_element_type=F32)
        o_ref[:nrows, :] = (a * _sigmoid(a) * b).astype(o_ref.dtype)
        if nrows < tile:
            o_ref[nrows:, :] = jnp.zeros((tile - nrows, o_ref.shape[1]), o_ref.dtype)

    rows = rows_ref[r]
    pl.when(rows > tile // 2)(functools.partial(compute, tile))
    pl.when(jnp.logical_and(rows > 0, rows <= tile // 2))(functools.partial(compute, tile // 2))

    @pl.when(rows == 0)
    def _():
        o_ref[...] = jnp.zeros_like(o_ref)

    @pl.when(r + 1 < n_active)
    def _():
        issue(r + 1, (r + 1) % 2, j * chunk, chunk)


def _moe_up(hp, slot_tok, group, w1, w3, lead, *, tile, tn, name):
    tile_exp, n_active, tile_rows = group
    n_slots = slot_tok.shape[0]
    half = hp.shape[1]
    d, f = w1.shape[-2], w1.shape[-1]
    gn = f // tn
    assert tile % gn == 0

    def w_map(r, j, e, na, rows, tok):
        re, je = _active_ij(r, j, na[0], gn)
        return (lead, e[re], 0, je)

    blk = 2 * _nbytes((d, tn), w1.dtype) + _nbytes((tile, tn), BF16)
    temp = (2 * _nbytes((tile, half), jnp.uint32) + _nbytes((tile, d), BF16)
            + 2 * _nbytes((d, tn), BF16) + 4 * tile * tn * 4)
    return pl.pallas_call(
        functools.partial(_moe_up_body, tile=tile, gn=gn),
        grid_spec=pltpu.PrefetchScalarGridSpec(
            num_scalar_prefetch=4, grid=(n_slots // tile, gn),
            in_specs=[pl.BlockSpec(memory_space=pl.ANY),
                      pl.BlockSpec((None, None, d, tn), w_map),
                      pl.BlockSpec((None, None, d, tn), w_map)],
            out_specs=pl.BlockSpec((tile, tn), lambda r, j, *pf: (r, j)),
            scratch_shapes=[pltpu.VMEM((2, tile, half), jnp.uint32), pltpu.VMEM((tile, d), BF16),
                            pltpu.SemaphoreType.DMA((2,))]),
        out_shape=jax.ShapeDtypeStruct((n_slots, f), BF16),
        compiler_params=pltpu.CompilerParams(
            dimension_semantics=("arbitrary", "arbitrary"), vmem_limit_bytes=_vmem_limit(blk, temp)),
        name=name,
    )(tile_exp, n_active, tile_rows, slot_tok, hp, w1, w3)


def _combine_body(slot_ref, x_ref, route_ref, g_ref, y_hbm, o_ref, buf, sem, *, rows, n_tiles, norm):
    i = pl.program_id(0)

    def issue(tile, slot):
        base = tile * rows

        def f(k, c):
            for j in range(TOP_K):
                _row_copy(y_hbm, buf.at[slot, j], slot_ref[(base + k) * TOP_K + j], k, sem.at[slot]).start()
            return c

        lax.fori_loop(0, rows, f, 0, unroll=4)

    @pl.when(i == 0)
    def _():
        issue(0, 0)

    @pl.when(i + 1 < n_tiles)
    def _():
        issue(i + 1, (i + 1) % 2)

    slot = i % 2

    def drain(k, c):
        for j in range(TOP_K):
            _row_copy(y_hbm, buf.at[slot, j], 0, k, sem.at[slot]).wait()
        return c

    lax.fori_loop(0, rows, drain, 0, unroll=4)
    route = route_ref[...]
    y = x_ref[...] + (buf[slot, 0] * route[:, 2:3] + buf[slot, 1] * route[:, 3:4])
    if norm:
        y = y * lax.rsqrt(jnp.mean(y * y, axis=-1, keepdims=True) + NORM_EPS) * g_ref[...]
    o_ref[...] = y


def _combine(x, route, slot_of_assign, yb, norm_gain, *, name):
    t, d = x.shape
    rows = _tile(t, 256, 8)
    n_tiles = t // rows
    norm = norm_gain is not None
    g = (norm_gain if norm else jnp.ones((d,), F32)).reshape(1, d).astype(F32)
    blk = 2 * _nbytes((rows, d), F32) + _nbytes((rows, V7X_LANES), F32)
    return pl.pallas_call(
        functools.partial(_combine_body, rows=rows, n_tiles=n_tiles, norm=norm),
        grid_spec=pltpu.PrefetchScalarGridSpec(
            num_scalar_prefetch=1, grid=(n_tiles,),
            in_specs=[pl.BlockSpec((rows, d), lambda i, s: (i, 0)),
                      pl.BlockSpec((rows, V7X_LANES), lambda i, s: (i, 0)),
                      pl.BlockSpec((1, d), lambda i, s: (0, 0)),
                      pl.BlockSpec(memory_space=pl.ANY)],
            out_specs=pl.BlockSpec((rows, d), lambda i, s: (i, 0)),
            scratch_shapes=[pltpu.VMEM((2, TOP_K, rows, d), F32), pltpu.SemaphoreType.DMA((2,))]),
        out_shape=jax.ShapeDtypeStruct((t, d), F32),
        compiler_params=pltpu.CompilerParams(
            dimension_semantics=("arbitrary",),
            vmem_limit_bytes=_vmem_limit(blk, (2 * TOP_K + 3) * _nbytes((rows, d), F32))),
        name=name,
    )(slot_of_assign, x, route, g, yb)


def _routing_tables(route, n_experts, tile):
    t = route.shape[0]
    a = t * TOP_K
    flat_e = route[:, :TOP_K].astype(jnp.int32).reshape(a)
    order = jnp.argsort(flat_e).astype(jnp.int32)
    rank = jnp.argsort(order).astype(jnp.int32)
    counts = jnp.sum(flat_e[:, None] == jnp.arange(n_experts, dtype=jnp.int32)[None, :], axis=0, dtype=jnp.int32)
    padded = (counts + tile - 1) // tile * tile
    pad_end = jnp.cumsum(padded)
    pad_start = pad_end - padded
    shift = pad_start - (jnp.cumsum(counts) - counts)
    slot_of_assign = rank + shift[flat_e]
    n_tiles = -(-a // tile) + n_experts
    tile_start = jnp.arange(n_tiles, dtype=jnp.int32) * tile
    tile_exp = jnp.minimum(jnp.searchsorted(pad_end, tile_start, side="right"),
                           n_experts - 1).astype(jnp.int32)
    slot = jnp.arange(n_tiles * tile, dtype=jnp.int32)
    slot_exp = jnp.repeat(tile_exp, tile)
    real = slot - pad_start[slot_exp] < counts[slot_exp]
    slot_tok = jnp.where(real, order[jnp.clip(slot - shift[slot_exp], 0, a - 1)] // TOP_K, 0)
    n_active = (pad_end[-1:] // tile).astype(jnp.int32)
    tile_rows = jnp.clip(counts[tile_exp] - (tile_start - pad_start[tile_exp]), 0, tile).astype(jnp.int32)
    return slot_tok.astype(jnp.int32), slot_of_assign.astype(jnp.int32), (tile_exp, n_active, tile_rows)


def _rope_panels(positions):
    inv_freq = ROPE_THETA ** (-jnp.arange(0, QK_ROPE, 2, dtype=F32) / QK_ROPE)
    ang = positions.astype(F32).reshape(-1, 1) * inv_freq
    cos, sin = jnp.cos(ang), jnp.sin(ang)
    z = jnp.zeros_like(cos)
    pad = jnp.zeros((cos.shape[0], V7X_LANES - QK_ROPE), F32)
    cos_t = jnp.concatenate([cos, cos, pad], axis=1)
    sin_a = jnp.concatenate([z, sin, pad], axis=1)
    sin_b = jnp.concatenate([-sin, z, pad], axis=1)
    return cos_t, sin_a, sin_b


def kernel(x, positions, mix_norm, w_in, q_norm, kv_norm, w_uq, w_ukv, v_norm_g, v_norm_b, w_sp, b_sp, w_branch_a, w_branch_b, w_out, ffn_norm, dense_w1, dense_w3, dense_w2, w_router, moe_w1, moe_w3, moe_w2, final_norm):
    batch, seq, d = x.shape
    t = batch * seq
    depth = mix_norm.shape[0]
    ql, kvl = q_norm.shape[1], kv_norm.shape[1]
    heads = w_ukv.shape[2] // (QK_NOPE + V_HEAD)
    gw = v_norm_g.shape[1]
    o2 = ql + kvl
    o3 = o2 + QK_ROPE
    o4 = o3 + 2 * gw
    assert ql % kvl == 0
    tm = _tile(t, 1024, 8)
    tn_d = _tile(d, 512)
    w_in_t = jnp.swapaxes(w_in, 1, 2)

    rope = _rope_panels(positions)
    rope_ex = [(p, "m", 0) for p in rope]
    xf = x.reshape(t, d)
    out = h = h_ss = None
    for layer in range(depth):
        w_q = jnp.pad(w_uq[layer].reshape(ql, heads, QK_NOPE + QK_ROPE),
                      ((0, 0), (0, 0), (0, HEAD_PAD - QK_NOPE - QK_ROPE))).reshape(ql, heads * HEAD_PAD).astype(BF16)

        if h is None:
            h, h_ss = _rmsnorm(xf, mix_norm[layer], BF16, name=f"mix_norm{layer}"), None
        cqkv = _mm([h], [W(0, w_in_t, layer, nt=True)], _epi_id, [], o2, F32,
                   tm=tm, tn=_tile(o2, 512), rowscale=h_ss, name=f"in_latent{layer}")
        k_rope = _mm([h], [W(0, w_in_t, layer, nt=True, row0=o2)], _epi_rope_all, rope_ex, V7X_LANES, BF16,
                     tm=tm, tn=V7X_LANES, rowscale=h_ss, name=f"in_krope{layer}")
        uv = _mm([h], [W(0, w_in_t, layer, nt=True, row0=o3)], _epi_gelu, [], 2 * gw, BF16,
                 tm=tm, tn=_tile(2 * gw, 512), rowscale=h_ss, name=f"in_uv{layer}")
        gates = _mm([h], [W(0, w_in_t, layer, nt=True, row0=o4)], _epi_sigmoid, [], 2 * d, BF16,
                    tm=tm, tn=_tile(2 * d, 512), rowscale=h_ss, name=f"in_gate{layer}")
        h = None

        q = _mm([cqkv], [W(0, w_q)], _epi_q_heads, rope_ex, heads * HEAD_PAD, BF16,
                tm=tm, tn=_tile(heads * HEAD_PAD, 2048, 2 * V7X_LANES),
                prenorm=(q_norm[layer], ql, 0), name=f"q_up{layer}")
        kv = _mm([cqkv], [W(0, w_ukv, layer)], _epi_id, [], heads * (QK_NOPE + V_HEAD), BF16,
                 tm=tm, tn=_tile(heads * (QK_NOPE + V_HEAD), 2048),
                 prenorm=(kv_norm[layer], kvl, ql // kvl), name=f"kv_up{layer}")
        y_a = _attention(q, kv, k_rope, batch=batch, seq=seq, heads=heads, name=f"attention{layer}")
        y_b = _gmlp(uv, v_norm_g[layer], v_norm_b[layer], w_sp[layer], b_sp[layer], name=f"gmlp{layer}")

        merged = _mm([y_a, y_b], [W(0, w_branch_a, layer), W(1, w_branch_b, layer)],
                     _epi_gated_merge, [(gates, "mn", 0), (gates, "mn", d // tn_d)], d, BF16,
                     tm=tm, tn=tn_d, name=f"merge{layer}")
        i = layer // 2
        last = layer == depth - 1
        dense = layer % 2 == 0
        xf = _mm([merged], [W(0, w_out, layer)], _epi_residual, [(xf, "mn", 0)], d, F32,
                 tm=tm, tn=tn_d, norm_out=ffn_norm[layer] if dense else None, name=f"mix_out{layer}")
        if dense:
            ff = dense_w1.shape[2]
            xf, hg, hg_ss = xf
            act, w2 = _mm([hg], [W(0, dense_w1, i), W(0, dense_w3, i)], _epi_swiglu, [], ff, BF16,
                          tm=tm, tn=_tile(ff, 256), rowscale=hg_ss, sidecast=dense_w2[i], name=f"dense_up{layer}")
            xf = _mm([act], [W(0, w2)], _epi_residual, [(xf, "mn", 0)], d, F32,
                     tm=_tile(t, 512, 8), tn=_tile(d, 512),
                     norm_out=None if last else mix_norm[layer + 1], name=f"dense_down{layer}")
            if last:
                out = _rmsnorm(xf, final_norm, F32, name="final_norm")
            else:
                xf, h, h_ss = xf
        else:
            n_experts = w_router.shape[2]
            fe = moe_w1.shape[3]
            hp, route = _norm_router(xf, ffn_norm[layer], w_router[i], name=f"ffn_norm_router{layer}")
            slot_tok, slot_of_assign, group = _routing_tables(route, n_experts, MOE_TILE)
            act = _moe_up(hp, slot_tok, group, moe_w1, moe_w3, i,
                          tile=MOE_TILE, tn=_tile(fe, 256), name=f"moe_up{layer}")
            yb = _mm([act], [W(0, moe_w2, i)], _epi_id, [], d, F32,
                     tm=MOE_TILE, tn=tn_d, group=group, name=f"moe_down{layer}")
            xf = _combine(xf, route, slot_of_assign, yb, final_norm if last else None, name=f"moe_combine{layer}")
            if last:
                out = xf
    return out.reshape(batch, seq, d)
```

```python
import functools
import math
from typing import Any, NamedTuple

import jax
import jax.numpy as jnp
from jax import lax
from jax.experimental import pallas as pl
from jax.experimental.pallas import tpu as pltpu

F32 = jnp.float32
BF16 = jnp.bfloat16

V7X_LANES = 128
V7X_VMEM_BYTES = 64 * 1024 * 1024
V7X_VMEM_CAP = V7X_VMEM_BYTES - 6 * 1024 * 1024

QK_NOPE = 128
QK_ROPE = 64
V_HEAD = 128
HEAD_PAD = 256
GROUP_DIM = 128
CHUNK = 128
TOP_K = 2
MOE_TILE = 1024
NORM_EPS = 1e-6
ROPE_THETA = 10000.0
Q_PRESCALE = float(QK_NOPE + QK_ROPE) ** -0.5 * math.log2(math.e)


def _tile(n, pref, unit=V7X_LANES):
    t = (min(pref, n) // unit) * unit
    while t >= unit:
        if n % t == 0:
            return t
        t -= unit
    return n


def _vmem_limit(block_bytes, temp_bytes):
    return int(min(V7X_VMEM_CAP, 2 * block_bytes + temp_bytes + (4 << 20)))


def _nbytes(shape, dtype):
    n = 1
    for s in shape:
        n *= s
    return n * jnp.dtype(dtype).itemsize


class W(NamedTuple):
    xi: int
    arr: Any
    lead: Any = None
    off: int = 0
    nt: bool = False
    row0: int = 0


_NT_DIMS = (((1,), (1,)), ((), ()))


def _mm_compute(x_refs, w_refs, e_refs, ss_ref, gain_ref, o_refs, ws, epi, inv_k, nrows=None):
    if nrows is not None:
        assert not e_refs and ss_ref is None and gain_ref is None
        o_ref = o_refs[0]
        x_refs = [x.at[pl.ds(0, nrows)] for x in x_refs]
        o_refs = [o_ref.at[pl.ds(0, nrows)]]
        o_ref[nrows:, :] = jnp.zeros((o_ref.shape[0] - nrows, o_ref.shape[1]), o_ref.dtype)
    accs = []
    for w, wr in zip(ws, w_refs):
        x = x_refs[w.xi][...]
        mat = wr[...].astype(BF16)
        if w.nt:
            accs.append(lax.dot_general(x, mat, _NT_DIMS, preferred_element_type=F32))
        else:
            accs.append(jnp.dot(x, mat, preferred_element_type=F32))
    if ss_ref is not None:
        r = lax.rsqrt(ss_ref[:, :1] * inv_k + NORM_EPS)
        accs = [a * r for a in accs]
    res = epi(accs, [e[...] for e in e_refs])
    o_refs[0][...] = res.astype(o_refs[0].dtype)
    if gain_ref is not None:
        xg_ref, sso_ref = o_refs[1], o_refs[2]
        xg_ref[...] = (res * gain_ref[...]).astype(xg_ref.dtype)
        part = jnp.broadcast_to(jnp.sum(res * res, axis=-1, keepdims=True), sso_ref.shape)
        j = pl.program_id(1)

        @pl.when(j == 0)
        def _():
            sso_ref[...] = part

        @pl.when(j > 0)
        def _():
            sso_ref[...] += part


def _mm_body(*refs, npf, nx, ws, ne, epi, prenorm, rowscale, norm_out, sidecast, inv_k):
    pf = refs[:npf]
    refs = refs[npf:]
    x_refs, refs = refs[:nx], refs[nx:]
    w_refs, refs = refs[:len(ws)], refs[len(ws):]
    e_refs, refs = refs[:ne], refs[ne:]
    ss_ref = gain_ref = None
    if rowscale:
        ss_ref, refs = refs[0], refs[1:]
    if norm_out:
        gain_ref, refs = refs[0], refs[1:]
    if sidecast:
        side_in, refs = refs[0], refs[1:]
    n_out = 3 if norm_out else 1
    o_refs, refs = refs[:n_out], refs[n_out:]
    if sidecast:
        side_out, refs = refs[0], refs[1:]
        side_out[...] = side_in[...].astype(side_out.dtype)
    if prenorm:
        xn_ref = refs[0]

        @pl.when(pl.program_id(1) == 0)
        def _():
            _rmsnorm_body(x_refs[0], x_refs[1], xn_ref)

        x_refs = (xn_ref,)
    compute = functools.partial(_mm_compute, x_refs, w_refs, e_refs, ss_ref, gain_ref, o_refs, ws, epi, inv_k)
    if npf:
        rows = pf[2][pl.program_id(0)]
        half = o_refs[0].shape[0] // 2
        pl.when(rows > half)(compute)
        pl.when(jnp.logical_and(rows > 0, rows <= half))(functools.partial(compute, nrows=half))

        @pl.when(rows == 0)
        def _():
            o_refs[0][...] = jnp.zeros_like(o_refs[0])
    else:
        compute()


def _active_ij(i, j, n_active, gn):
    return jnp.minimum(i, n_active - 1), jnp.where(i < n_active, j, gn - 1)


def _mm(xs, ws, epi, extras, n, out_dtype, *, tm, tn, group=None, prenorm=None, rowscale=None, norm_out=None,
        sidecast=None, name):
    m = xs[0].shape[0]
    gm, gn = m // tm, n // tn
    npf = 0 if group is None else 3

    def imap(f):
        if group is None:
            return f
        return lambda i, j, g, na, rows: f(*_active_ij(i, j, na[0], gn), g)

    in_specs = []
    blk = 0
    temp = (len(ws) + 2) * tm * tn * 4
    scratch = []
    if prenorm is None:
        for x in xs:
            in_specs.append(pl.BlockSpec((tm, x.shape[1]), imap(lambda i, j, *pf: (i, 0))))
            blk += _nbytes((tm, x.shape[1]), x.dtype)
    else:
        gain, width, col_block = prenorm
        assert len(xs) == 1 and group is None
        in_specs.append(pl.BlockSpec((tm, width), lambda i, j: (i, col_block)))
        in_specs.append(pl.BlockSpec((1, width), lambda i, j: (0, 0)))
        xs = [xs[0], gain.reshape(1, width).astype(F32)]
        blk += _nbytes((tm, width), F32)
        scratch.append(pltpu.VMEM((tm, width), BF16))
        temp += _nbytes((tm, width), BF16) + 2 * _nbytes((tm, width), F32)
    for w in ws:
        lead = () if w.lead is None else (w.lead,)
        none = (None,) * len(lead)
        if w.nt:
            assert group is None and w.arr.ndim == 3
            k = w.arr.shape[-1]
            in_specs.append(pl.BlockSpec((pl.Squeezed(), pl.Element(tn), pl.Element(k)), imap(
                lambda i, j, *pf, lead=w.lead, row0=w.row0: (lead, pl.multiple_of(row0 + j * tn, 8), 0))))
        elif group is None:
            assert w.arr.ndim == 2 + len(lead)
            k = w.arr.shape[-2]
            in_specs.append(pl.BlockSpec(none + (k, tn), imap(
                lambda i, j, *pf, lead=lead, off=w.off: lead + (0, j + off))))
        else:
            assert w.arr.ndim == 3 + len(lead)
            k = w.arr.shape[-2]
            in_specs.append(pl.BlockSpec(none + (None, k, tn), imap(
                lambda i, j, g, lead=lead: lead + (g[i], 0, j))))
        blk += _nbytes((k, tn), w.arr.dtype)
        if w.arr.dtype != BF16:
            temp += _nbytes((k, tn), BF16)
    for arr, kind, off in extras:
        if kind == "mn":
            in_specs.append(pl.BlockSpec((tm, tn), imap(lambda i, j, *pf, off=off: (i, j + off))))
            blk += _nbytes((tm, tn), arr.dtype)
        else:
            in_specs.append(pl.BlockSpec((tm, arr.shape[1]), imap(lambda i, j, *pf: (i, 0))))
            blk += _nbytes((tm, arr.shape[1]), arr.dtype)
    tail = []
    if rowscale is not None:
        assert group is None
        in_specs.append(pl.BlockSpec((tm, V7X_LANES), lambda i, j: (i, 0)))
        blk += _nbytes((tm, V7X_LANES), F32)
        tail.append(rowscale)
    out_specs = [pl.BlockSpec((tm, tn), lambda i, j, *pf: (i, j))]
    out_shape = [jax.ShapeDtypeStruct((m, n), out_dtype)]
    blk += _nbytes((tm, tn), out_dtype)
    if norm_out is not None:
        assert group is None
        in_specs.append(pl.BlockSpec((1, tn), lambda i, j: (0, j)))
        tail.append(norm_out.reshape(1, n).astype(F32))
        out_specs += [pl.BlockSpec((tm, tn), lambda i, j: (i, j)), pl.BlockSpec((tm, V7X_LANES), lambda i, j: (i, 0))]
        out_shape += [jax.ShapeDtypeStruct((m, n), BF16), jax.ShapeDtypeStruct((m, V7X_LANES), F32)]
        blk += _nbytes((tm, tn), BF16) + _nbytes((tm, V7X_LANES), F32)
    if sidecast is not None:
        assert group is None and sidecast.shape[0] % (gm * gn) == 0
        side_blk = (sidecast.shape[0] // (gm * gn), sidecast.shape[1])
        in_specs.append(pl.BlockSpec(side_blk, lambda i, j: (i * gn + j, 0)))
        tail.append(sidecast)
        out_specs.append(pl.BlockSpec(side_blk, lambda i, j: (i * gn + j, 0)))
        out_shape.append(jax.ShapeDtypeStruct(sidecast.shape, BF16))
        blk += _nbytes(side_blk, sidecast.dtype) + _nbytes(side_blk, BF16)
    inv_k = 1.0 / xs[0].shape[1]
    body = functools.partial(_mm_body, npf=npf, nx=len(xs), ws=tuple(w._replace(arr=None) for w in ws),
                             ne=len(extras), epi=epi, prenorm=prenorm is not None,
                             rowscale=rowscale is not None, norm_out=norm_out is not None,
                             sidecast=sidecast is not None, inv_k=inv_k)
    call = pl.pallas_call(
        body,
        grid_spec=pltpu.PrefetchScalarGridSpec(
            num_scalar_prefetch=npf, grid=(gm, gn), in_specs=in_specs, out_specs=out_specs,
            scratch_shapes=scratch),
        out_shape=out_shape,
        compiler_params=pltpu.CompilerParams(
            dimension_semantics=("arbitrary", "arbitrary"), vmem_limit_bytes=_vmem_limit(blk, temp)),
        name=name,
    )
    args = [] if group is None else list(group)
    args += list(xs) + [w.arr for w in ws] + [a for a, _, _ in extras] + tail
    res = call(*args)
    return res if len(res) > 1 else res[0]


def _epi_id(accs, ex):
    return accs[0]


def _epi_gelu(accs, ex):
    a = accs[0]
    return 0.5 * a * (1.0 + lax.erf(a * (2.0 ** -0.5)))


def _sigmoid(a):
    return 0.5 * (1.0 + jnp.tanh(0.5 * a))


def _epi_sigmoid(accs, ex):
    return _sigmoid(accs[0])


def _epi_residual(accs, ex):
    return ex[0] + accs[0]


def _epi_swiglu(accs, ex):
    return accs[0] * _sigmoid(accs[0]) * accs[1]


def _epi_gated_merge(accs, ex):
    return ex[0].astype(F32) * accs[0] + ex[1].astype(F32) * accs[1]


def _rope_lanes(t, cos_t, sin_a, sin_b):
    half = QK_ROPE // 2
    return (t * cos_t + pltpu.roll(t, half, 1) * sin_a
            + pltpu.roll(t, V7X_LANES - half, 1) * sin_b)


def _epi_rope_all(accs, ex):
    return _rope_lanes(accs[0], *ex)


def _epi_q_heads(accs, ex):
    a = accs[0]
    outs = []
    for g in range(a.shape[1] // V7X_LANES):
        t = a[:, g * V7X_LANES:(g + 1) * V7X_LANES]
        outs.append((_rope_lanes(t, *ex) if g % 2 else t) * Q_PRESCALE)
    return jnp.concatenate(outs, axis=1)


def _rmsnorm_body(x_ref, g_ref, o_ref):
    x = x_ref[...].astype(F32)
    y = x * lax.rsqrt(jnp.mean(x * x, axis=-1, keepdims=True) + NORM_EPS)
    o_ref[...] = (y * g_ref[...]).astype(o_ref.dtype)


def _rmsnorm(x, g, out_dtype, *, width=None, col_block=0, name):
    m = x.shape[0]
    width = x.shape[1] if width is None else width
    tm = _tile(m, 256, 8)
    blk = _nbytes((tm, width), x.dtype) + _nbytes((tm, width), out_dtype)
    return pl.pallas_call(
        _rmsnorm_body,
        grid=(m // tm,),
        in_specs=[pl.BlockSpec((tm, width), lambda i: (i, col_block)),
                  pl.BlockSpec((1, width), lambda i: (0, 0))],
        out_specs=pl.BlockSpec((tm, width), lambda i: (i, 0)),
        out_shape=jax.ShapeDtypeStruct((m, width), out_dtype),
        compiler_params=pltpu.CompilerParams(
            dimension_semantics=("arbitrary",), vmem_limit_bytes=_vmem_limit(blk, 3 * tm * width * 4)),
        name=name,
    )(x, g.reshape(1, width).astype(F32))


def _pack_bf16_pairs(h):
    half = h.shape[1] // 2
    bits = lax.bitcast_convert_type(h.astype(BF16).astype(F32), jnp.uint32)
    return bits[:, :half] | (bits[:, half:] >> 16)


def _unpack_bf16_pairs(p):
    hi = lax.bitcast_convert_type(p & jnp.uint32(0xFFFF0000), F32).astype(BF16)
    lo = lax.bitcast_convert_type(p << 16, F32).astype(BF16)
    return hi, lo


def _norm_router_body(x_ref, g_ref, wr_ref, h_ref, route_ref, *, n_experts):
    x = x_ref[...]
    h = x * lax.rsqrt(jnp.mean(x * x, axis=-1, keepdims=True) + NORM_EPS) * g_ref[...]
    h_ref[...] = _pack_bf16_pairs(h)
    wr = wr_ref[...]
    h_hi, w_hi = h.astype(BF16), wr.astype(BF16)
    h_lo = (h - h_hi.astype(F32)).astype(BF16)
    w_lo = (wr - w_hi.astype(F32)).astype(BF16)
    logits = (jnp.dot(h_hi, w_hi, preferred_element_type=F32)
              + (jnp.dot(h_lo, w_hi, preferred_element_type=F32) + jnp.dot(h_hi, w_lo, preferred_element_type=F32)))
    lane = lax.broadcasted_iota(jnp.int32, logits.shape, 1).astype(F32)
    neg = jnp.float32(-jnp.inf)
    far = jnp.float32(V7X_LANES)
    l1 = jnp.where(lane < n_experts, logits, neg)
    m1 = jnp.max(l1, axis=-1, keepdims=True)
    i1 = jnp.min(jnp.where(l1 == m1, lane, far), axis=-1, keepdims=True)
    l2 = jnp.where(lane == i1, neg, l1)
    m2 = jnp.max(l2, axis=-1, keepdims=True)
    i2 = jnp.min(jnp.where(l2 == m2, lane, far), axis=-1, keepdims=True)
    e = jnp.exp(m2 - m1)
    g1 = 1.0 / (1.0 + e)
    g2 = e / (1.0 + e)
    route = jnp.where(lane == 0, i1, jnp.where(lane == 1, i2, jnp.where(lane == 2, g1, jnp.where(lane == 3, g2, 0.0))))
    route_ref[...] = route


def _norm_router(x, g, w_router, *, name):
    m, d = x.shape
    n_experts = w_router.shape[1]
    wr = jnp.pad(w_router.astype(F32), ((0, 0), (0, V7X_LANES - n_experts)))
    tm = _tile(m, 256, 8)
    blk = 2 * _nbytes((tm, d), F32) + _nbytes((d, V7X_LANES), F32) + _nbytes((tm, V7X_LANES), F32)
    return pl.pallas_call(
        functools.partial(_norm_router_body, n_experts=n_experts),
        grid=(m // tm,),
        in_specs=[pl.BlockSpec((tm, d), lambda i: (i, 0)),
                  pl.BlockSpec((1, d), lambda i: (0, 0)),
                  pl.BlockSpec((d, V7X_LANES), lambda i: (0, 0))],
        out_specs=[pl.BlockSpec((tm, d // 2), lambda i: (i, 0)),
                   pl.BlockSpec((tm, V7X_LANES), lambda i: (i, 0))],
        out_shape=[jax.ShapeDtypeStruct((m, d // 2), jnp.uint32), jax.ShapeDtypeStruct((m, V7X_LANES), F32)],
        compiler_params=pltpu.CompilerParams(
            dimension_semantics=("arbitrary",), vmem_limit_bytes=_vmem_limit(blk, 4 * tm * d * 4)),
        name=name,
    )(x, g.reshape(1, d).astype(F32), wr)


def _attn_body(q_ref, kn_ref, kr_ref, v_ref, o_ref, kcat_ref, vone_ref, *, sub):
    @pl.when(pl.program_id(2) == 0)
    def _():
        kcat_ref[:, :QK_NOPE] = kn_ref[...]
        kcat_ref[:, QK_NOPE:] = kr_ref[...]
        vone_ref[:, :V_HEAD] = v_ref[...]
        vone_ref[:, V_HEAD:] = jnp.ones((v_ref.shape[0], V_HEAD), vone_ref.dtype)

    for r0 in range(0, q_ref.shape[0], sub):
        rows = slice(r0, r0 + sub)
        s = lax.dot_general(q_ref[rows, :], kcat_ref[...], (((1,), (1,)), ((), ())),
                            preferred_element_type=F32)
        p = jnp.exp2(s - jnp.max(s, axis=-1, keepdims=True))
        ol = jnp.dot(p.astype(BF16), vone_ref[...], preferred_element_type=F32)
        o_ref[rows, :] = (ol[:, :V_HEAD] / ol[:, V_HEAD:]).astype(o_ref.dtype)


def _attention(q, kv, k_rope, *, batch, seq, heads, name):
    t = batch * seq
    sub = _tile(seq, 256, 8)
    tq = _tile(seq, 8 * sub, sub)
    nq = seq // tq
    blk = (_nbytes((tq, HEAD_PAD), BF16) + 3 * _nbytes((seq, V7X_LANES), BF16) + _nbytes((tq, V_HEAD), BF16))
    return pl.pallas_call(
        functools.partial(_attn_body, sub=sub),
        grid=(batch, heads, nq),
        in_specs=[pl.BlockSpec((tq, HEAD_PAD), lambda b, h, i: (b * nq + i, h)),
                  pl.BlockSpec((seq, QK_NOPE), lambda b, h, i: (b, 2 * h)),
                  pl.BlockSpec((seq, V7X_LANES), lambda b, h, i: (b, 0)),
                  pl.BlockSpec((seq, V_HEAD), lambda b, h, i: (b, 2 * h + 1))],
        out_specs=pl.BlockSpec((tq, V_HEAD), lambda b, h, i: (b * nq + i, h)),
        out_shape=jax.ShapeDtypeStruct((t, heads * V_HEAD), BF16),
        scratch_shapes=[pltpu.VMEM((seq, HEAD_PAD), BF16), pltpu.VMEM((seq, 2 * V_HEAD), BF16)],
        compiler_params=pltpu.CompilerParams(
            dimension_semantics=("arbitrary", "arbitrary", "arbitrary"),
            vmem_limit_bytes=_vmem_limit(blk, _nbytes((seq, HEAD_PAD), BF16) + 4 * tq * seq * 4)),
        name=name,
    )(q, kv, k_rope, kv)


def _gmlp_body(u_ref, v_ref, g_ref, b_ref, w_ref, bs_ref, o_ref, *, n_chunks, n_groups):
    v = v_ref[...].astype(F32)
    mu = jnp.mean(v, axis=-1, keepdims=True)
    vc = v - mu
    vn = vc * lax.rsqrt(jnp.mean(vc * vc, axis=-1, keepdims=True) + NORM_EPS)
    vn = (vn * g_ref[...] + b_ref[...]).astype(BF16)
    for c in range(n_chunks):
        rows = slice(c * CHUNK, (c + 1) * CHUNK)
        for g in range(n_groups):
            cols = slice(g * GROUP_DIM, (g + 1) * GROUP_DIM)
            s = jnp.dot(w_ref[g], vn[rows, cols], preferred_element_type=F32) + bs_ref[g]
            o_ref[rows, cols] = (u_ref[rows, cols].astype(F32) * s).astype(o_ref.dtype)


def _gmlp(uv, v_norm_g, v_norm_b, w_sp, b_sp, *, name):
    t = uv.shape[0]
    gw = uv.shape[1] // 2
    n_groups = w_sp.shape[0]
    rows = _tile(t, 2 * CHUNK, CHUNK)
    b_full = jnp.broadcast_to(b_sp.astype(F32)[:, :, None], (n_groups, CHUNK, GROUP_DIM))
    blk = 3 * _nbytes((rows, gw), BF16) + _nbytes(w_sp.shape, BF16) + _nbytes(b_full.shape, F32)
    return pl.pallas_call(
        functools.partial(_gmlp_body, n_chunks=rows // CHUNK, n_groups=n_groups),
        grid=(t // rows,),
        in_specs=[pl.BlockSpec((rows, gw), lambda i: (i, 0)),
                  pl.BlockSpec((rows, gw), lambda i: (i, 1)),
                  pl.BlockSpec((1, gw), lambda i: (0, 0)),
                  pl.BlockSpec((1, gw), lambda i: (0, 0)),
                  pl.BlockSpec((n_groups, CHUNK, CHUNK), lambda i: (0, 0, 0)),
                  pl.BlockSpec((n_groups, CHUNK, GROUP_DIM), lambda i: (0, 0, 0))],
        out_specs=pl.BlockSpec((rows, gw), lambda i: (i, 0)),
        out_shape=jax.ShapeDtypeStruct((t, gw), BF16),
        compiler_params=pltpu.CompilerParams(
            dimension_semantics=("arbitrary",), vmem_limit_bytes=_vmem_limit(blk, 4 * rows * gw * 4)),
        name=name,
    )(uv, uv, v_norm_g.reshape(1, gw).astype(F32), v_norm_b.reshape(1, gw).astype(F32),
      w_sp.astype(BF16), b_full)


def _row_copy(src_hbm, dst_vmem, src_row, dst_row, sem):
    return pltpu.make_async_copy(src_hbm.at[pl.ds(src_row, 1)], dst_vmem.at[pl.ds(dst_row, 1)], sem)


def _moe_up_body(exp_ref, nact_ref, rows_ref, tok_ref, hp_hbm, w1_ref, w3_ref, o_ref, xbuf, xbf, sem, *, tile, gn):
    r = pl.program_id(0)
    j = pl.program_id(1)
    n_active = nact_ref[0]
    chunk = tile // gn
    half = xbuf.shape[2]

    def issue(row_tile, slot, row0, n_rows):
        base = row_tile * tile

        def f(k, c):
            row = row0 + k
            _row_copy(hp_hbm, xbuf.at[slot], tok_ref[base + row], row, sem.at[slot]).start()
            return c

        lax.fori_loop(0, n_rows, f, 0, unroll=8)

    @pl.when(jnp.logical_and(r == 0, j == 0))
    def _():
        issue(0, 0, 0, tile)

    @pl.when(jnp.logical_and(r < n_active, j == 0))
    def _():
        slot = r % 2

        def drain(k, c):
            _row_copy(hp_hbm, xbuf.at[slot], 0, k, sem.at[slot]).wait()
            return c

        lax.fori_loop(0, tile, drain, 0, unroll=8)
        hi, lo = _unpack_bf16_pairs(xbuf[slot])
        xbf[:, :half] = hi
        xbf[:, half:] = lo

    def compute(nrows):
        x = xbf[:nrows, :]
        a = jnp.dot(x, w1_ref[...].astype(BF16), preferred_element_type=F32)
        b = jnp.dot(x, w3_ref[...].astype(BF16), preferred_element_type=F32)
        o_ref[:nrows, :] = (a * _sigmoid(a) * b).astype(o_ref.dtype)
        if nrows < tile:
            o_ref[nrows:, :] = jnp.zeros((tile - nrows, o_ref.shape[1]), o_ref.dtype)

    rows = rows_ref[r]
    pl.when(rows > tile // 2)(functools.partial(compute, tile))
    pl.when(jnp.logical_and(rows > 0, rows <= tile // 2))(functools.partial(compute, tile // 2))

    @pl.when(rows == 0)
    def _():
        o_ref[...] = jnp.zeros_like(o_ref)

    @pl.when(r + 1 < n_active)
    def _():
        issue(r + 1, (r + 1) % 2, j * chunk, chunk)


def _moe_up(hp, slot_tok, group, w1, w3, lead, *, tile, tn, name):
    tile_exp, n_active, tile_rows = group
    n_slots = slot_tok.shape[0]
    half = hp.shape[1]
    d, f = w1.shape[-2], w1.shape[-1]
    gn = f // tn
    assert tile % gn == 0

    def w_map(r, j, e, na, rows, tok):
        re, je = _active_ij(r, j, na[0], gn)
        return (lead, e[re], 0, je)

    blk = 2 * _nbytes((d, tn), w1.dtype) + _nbytes((tile, tn), BF16)
    temp = (2 * _nbytes((tile, half), jnp.uint32) + _nbytes((tile, d), BF16)
            + 2 * _nbytes((d, tn), BF16) + 4 * tile * tn * 4)
    return pl.pallas_call(
        functools.partial(_moe_up_body, tile=tile, gn=gn),
        grid_spec=pltpu.PrefetchScalarGridSpec(
            num_scalar_prefetch=4, grid=(n_slots // tile, gn),
            in_specs=[pl.BlockSpec(memory_space=pl.ANY),
                      pl.BlockSpec((None, None, d, tn), w_map),
                      pl.BlockSpec((None, None, d, tn), w_map)],
            out_specs=pl.BlockSpec((tile, tn), lambda r, j, *pf: (r, j)),
            scratch_shapes=[pltpu.VMEM((2, tile, half), jnp.uint32), pltpu.VMEM((tile, d), BF16),
                            pltpu.SemaphoreType.DMA((2,))]),
        out_shape=jax.ShapeDtypeStruct((n_slots, f), BF16),
        compiler_params=pltpu.CompilerParams(
            dimension_semantics=("arbitrary", "arbitrary"), vmem_limit_bytes=_vmem_limit(blk, temp)),
        name=name,
    )(tile_exp, n_active, tile_rows, slot_tok, hp, w1, w3)


def _combine_body(slot_ref, x_ref, route_ref, g_ref, y_hbm, o_ref, buf, sem, *, rows, n_tiles, norm):
    i = pl.program_id(0)

    def issue(tile, slot):
        base = tile * rows

        def f(k, c):
            for j in range(TOP_K):
                _row_copy(y_hbm, buf.at[slot, j], slot_ref[(base + k) * TOP_K + j], k, sem.at[slot]).start()
            return c

        lax.fori_loop(0, rows, f, 0, unroll=4)

    @pl.when(i == 0)
    def _():
        issue(0, 0)

    @pl.when(i + 1 < n_tiles)
    def _():
        issue(i + 1, (i + 1) % 2)

    slot = i % 2

    def drain(k, c):
        for j in range(TOP_K):
            _row_copy(y_hbm, buf.at[slot, j], 0, k, sem.at[slot]).wait()
        return c

    lax.fori_loop(0, rows, drain, 0, unroll=4)
    route = route_ref[...]
    y = x_ref[...] + (buf[slot, 0] * route[:, 2:3] + buf[slot, 1] * route[:, 3:4])
    if norm:
        y = y * lax.rsqrt(jnp.mean(y * y, axis=-1, keepdims=True) + NORM_EPS) * g_ref[...]
    o_ref[...] = y


def _combine(x, route, slot_of_assign, yb, norm_gain, *, name):
    t, d = x.shape
    rows = _tile(t, 256, 8)
    n_tiles = t // rows
    norm = norm_gain is not None
    g = (norm_gain if norm else jnp.ones((d,), F32)).reshape(1, d).astype(F32)
    blk = 2 * _nbytes((rows, d), F32) + _nbytes((rows, V7X_LANES), F32)
    return pl.pallas_call(
        functools.partial(_combine_body, rows=rows, n_tiles=n_tiles, norm=norm),
        grid_spec=pltpu.PrefetchScalarGridSpec(
            num_scalar_prefetch=1, grid=(n_tiles,),
            in_specs=[pl.BlockSpec((rows, d), lambda i, s: (i, 0)),
                      pl.BlockSpec((rows, V7X_LANES), lambda i, s: (i, 0)),
                      pl.BlockSpec((1, d), lambda i, s: (0, 0)),
                      pl.BlockSpec(memory_space=pl.ANY)],
            out_specs=pl.BlockSpec((rows, d), lambda i, s: (i, 0)),
            scratch_shapes=[pltpu.VMEM((2, TOP_K, rows, d), F32), pltpu.SemaphoreType.DMA((2,))]),
        out_shape=jax.ShapeDtypeStruct((t, d), F32),
        compiler_params=pltpu.CompilerParams(
            dimension_semantics=("arbitrary",),
            vmem_limit_bytes=_vmem_limit(blk, (2 * TOP_K + 3) * _nbytes((rows, d), F32))),
        name=name,
    )(slot_of_assign, x, route, g, yb)


def _routing_tables(route, n_experts, tile):
    t = route.shape[0]
    a = t * TOP_K
    flat_e = route[:, :TOP_K].astype(jnp.int32).reshape(a)
    order = jnp.argsort(flat_e).astype(jnp.int32)
    rank = jnp.argsort(order).astype(jnp.int32)
    counts = jnp.sum(flat_e[:, None] == jnp.arange(n_experts, dtype=jnp.int32)[None, :], axis=0, dtype=jnp.int32)
    padded = (counts + tile - 1) // tile * tile
    pad_end = jnp.cumsum(padded)
    pad_start = pad_end - padded
    shift = pad_start - (jnp.cumsum(counts) - counts)
    slot_of_assign = rank + shift[flat_e]
    n_tiles = -(-a // tile) + n_experts
    tile_start = jnp.arange(n_tiles, dtype=jnp.int32) * tile
    tile_exp = jnp.minimum(jnp.searchsorted(pad_end, tile_start, side="right"),
                           n_experts - 1).astype(jnp.int32)
    slot = jnp.arange(n_tiles * tile, dtype=jnp.int32)
    slot_exp = jnp.repeat(tile_exp, tile)
    real = slot - pad_start[slot_exp] < counts[slot_exp]
    slot_tok = jnp.where(real, order[jnp.clip(slot - shift[slot_exp], 0, a - 1)] // TOP_K, 0)
    n_active = (pad_end[-1:] // tile).astype(jnp.int32)
    tile_rows = jnp.clip(counts[tile_exp] - (tile_start - pad_start[tile_exp]), 0, tile).astype(jnp.int32)
    return slot_tok.astype(jnp.int32), slot_of_assign.astype(jnp.int32), (tile_exp, n_active, tile_rows)


def _rope_panels(positions):
    inv_freq = ROPE_THETA ** (-jnp.arange(0, QK_ROPE, 2, dtype=F32) / QK_ROPE)
    ang = positions.astype(F32).reshape(-1, 1) * inv_freq
    cos, sin = jnp.cos(ang), jnp.sin(ang)
    z = jnp.zeros_like(cos)
    pad = jnp.zeros((cos.shape[0], V7X_LANES - QK_ROPE), F32)
    cos_t = jnp.concatenate([cos, cos, pad], axis=1)
    sin_a = jnp.concatenate([z, sin, pad], axis=1)
    sin_b = jnp.concatenate([-sin, z, pad], axis=1)
    return cos_t, sin_a, sin_b


def kernel(x, positions, mix_norm, w_in, q_norm, kv_norm, w_uq, w_ukv, v_norm_g, v_norm_b, w_sp, b_sp, w_branch_a, w_branch_b, w_out, ffn_norm, dense_w1, dense_w3, dense_w2, w_router, moe_w1, moe_w3, moe_w2, final_norm):
    batch, seq, d = x.shape
    t = batch * seq
    depth = mix_norm.shape[0]
    ql, kvl = q_norm.shape[1], kv_norm.shape[1]
    heads = w_ukv.shape[2] // (QK_NOPE + V_HEAD)
    gw = v_norm_g.shape[1]
    o2 = ql + kvl
    o3 = o2 + QK_ROPE
    o4 = o3 + 2 * gw
    assert ql % kvl == 0
    tm = _tile(t, 1024, 8)
    tn_d = _tile(d, 512)
    w_in_t = jnp.swapaxes(w_in, 1, 2)

    rope = _rope_panels(positions)
    rope_ex = [(p, "m", 0) for p in rope]
    xf = x.reshape(t, d)
    out = h = h_ss = None
    for layer in range(depth):
        w_q = jnp.pad(w_uq[layer].reshape(ql, heads, QK_NOPE + QK_ROPE),
                      ((0, 0), (0, 0), (0, HEAD_PAD - QK_NOPE - QK_ROPE))).reshape(ql, heads * HEAD_PAD).astype(BF16)

        if h is None:
            h, h_ss = _rmsnorm(xf, mix_norm[layer], BF16, name=f"mix_norm{layer}"), None
        cqkv = _mm([h], [W(0, w_in_t, layer, nt=True)], _epi_id, [], o2, F32,
                   tm=tm, tn=_tile(o2, 512), rowscale=h_ss, name=f"in_latent{layer}")
        k_rope = _mm([h], [W(0, w_in_t, layer, nt=True, row0=o2)], _epi_rope_all, rope_ex, V7X_LANES, BF16,
                     tm=tm, tn=V7X_LANES, rowscale=h_ss, name=f"in_krope{layer}")
        uv = _mm([h], [W(0, w_in_t, layer, nt=True, row0=o3)], _epi_gelu, [], 2 * gw, BF16,
                 tm=tm, tn=_tile(2 * gw, 512), rowscale=h_ss, name=f"in_uv{layer}")
        gates = _mm([h], [W(0, w_in_t, layer, nt=True, row0=o4)], _epi_sigmoid, [], 2 * d, BF16,
                    tm=tm, tn=_tile(2 * d, 512), rowscale=h_ss, name=f"in_gate{layer}")
        h = None

        q = _mm([cqkv], [W(0, w_q)], _epi_q_heads, rope_ex, heads * HEAD_PAD, BF16,
                tm=tm, tn=_tile(heads * HEAD_PAD, 2048, 2 * V7X_LANES),
                prenorm=(q_norm[layer], ql, 0), name=f"q_up{layer}")
        kv = _mm([cqkv], [W(0, w_ukv[layer].astype(BF16))], _epi_id, [], heads * (QK_NOPE + V_HEAD), BF16,
                 tm=tm, tn=_tile(heads * (QK_NOPE + V_HEAD), 2048),
                 prenorm=(kv_norm[layer], kvl, ql // kvl), name=f"kv_up{layer}")
        y_a = _attention(q, kv, k_rope, batch=batch, seq=seq, heads=heads, name=f"attention{layer}")
        y_b = _gmlp(uv, v_norm_g[layer], v_norm_b[layer], w_sp[layer], b_sp[layer], name=f"gmlp{layer}")

        merged = _mm([y_a, y_b], [W(0, w_branch_a, layer), W(1, w_branch_b, layer)],
                     _epi_gated_merge, [(gates, "mn", 0), (gates, "mn", d // tn_d)], d, BF16,
                     tm=tm, tn=tn_d, name=f"merge{layer}")
        i = layer // 2
        last = layer == depth - 1
        dense = layer % 2 == 0
        xf = _mm([merged], [W(0, w_out, layer)], _epi_residual, [(xf, "mn", 0)], d, F32,
                 tm=tm, tn=tn_d, norm_out=ffn_norm[layer] if dense else None, name=f"mix_out{layer}")
        if dense:
            ff = dense_w1.shape[2]
            xf, hg, hg_ss = xf
            act, w2 = _mm([hg], [W(0, dense_w1, i), W(0, dense_w3, i)], _epi_swiglu, [], ff, BF16,
                          tm=tm, tn=_tile(ff, 256), rowscale=hg_ss, sidecast=dense_w2[i], name=f"dense_up{layer}")
            xf = _mm([act], [W(0, w2)], _epi_residual, [(xf, "mn", 0)], d, F32,
                     tm=_tile(t, 512, 8), tn=_tile(d, 512),
                     norm_out=None if last else mix_norm[layer + 1], name=f"dense_down{layer}")
            if last:
                out = _rmsnorm(xf, final_norm, F32, name="final_norm")
            else:
                xf, h, h_ss = xf
        else:
            n_experts = w_router.shape[2]
            fe = moe_w1.shape[3]
            hp, route = _norm_router(xf, ffn_norm[layer], w_router[i], name=f"ffn_norm_router{layer}")
            slot_tok, slot_of_assign, group = _routing_tables(route, n_experts, MOE_TILE)
            act = _moe_up(hp, slot_tok, group, moe_w1, moe_w3, i,
                          tile=MOE_TILE, tn=_tile(fe, 256), name=f"moe_up{layer}")
            yb = _mm([act], [W(0, moe_w2, i)], _epi_id, [], d, F32,
                     tm=MOE_TILE, tn=tn_d, group=group, name=f"moe_down{layer}")
            xf = _combine(xf, route, slot_of_assign, yb, final_norm if last else None, name=f"moe_combine{layer}")
            if last:
                out = xf
    return out.reshape(batch, seq, d)
```

```python
import functools
import math
from typing import Any, NamedTuple

import jax
import jax.numpy as jnp
from jax import lax
from jax.experimental import pallas as pl
from jax.experimental.pallas import tpu as pltpu

F32 = jnp.float32
BF16 = jnp.bfloat16

V7X_LANES = 128
V7X_VMEM_BYTES = 64 * 1024 * 1024
V7X_VMEM_CAP = V7X_VMEM_BYTES - 6 * 1024 * 1024

QK_NOPE = 128
QK_ROPE = 64
V_HEAD = 128
HEAD_PAD = 256
GROUP_DIM = 128
CHUNK = 128
TOP_K = 2
MOE_TILE = 1024
NORM_EPS = 1e-6
ROPE_THETA = 10000.0
Q_PRESCALE = float(QK_NOPE + QK_ROPE) ** -0.5 * math.log2(math.e)


def _tile(n, pref, unit=V7X_LANES):
    t = (min(pref, n) // unit) * unit
    while t >= unit:
        if n % t == 0:
            return t
        t -= unit
    return n


def _vmem_limit(block_bytes, temp_bytes):
    return int(min(V7X_VMEM_CAP, 2 * block_bytes + temp_bytes + (4 << 20)))


def _nbytes(shape, dtype):
    n = 1
    for s in shape:
        n *= s
    return n * jnp.dtype(dtype).itemsize


class W(NamedTuple):
    xi: int
    arr: Any
    lead: Any = None
    off: int = 0
    nt: bool = False
    row0: int = 0


_NT_DIMS = (((1,), (1,)), ((), ()))


def _mm_compute(x_refs, w_refs, e_refs, ss_ref, gain_ref, o_refs, ws, epi, inv_k, nrows=None):
    if nrows is not None:
        assert not e_refs and ss_ref is None and gain_ref is None
        o_ref = o_refs[0]
        x_refs = [x.at[pl.ds(0, nrows)] for x in x_refs]
        o_refs = [o_ref.at[pl.ds(0, nrows)]]
        o_ref[nrows:, :] = jnp.zeros((o_ref.shape[0] - nrows, o_ref.shape[1]), o_ref.dtype)
    accs = []
    for w, wr in zip(ws, w_refs):
        x = x_refs[w.xi][...]
        mat = wr[...].astype(BF16)
        if w.nt:
            accs.append(lax.dot_general(x, mat, _NT_DIMS, preferred_element_type=F32))
        else:
            accs.append(jnp.dot(x, mat, preferred_element_type=F32))
    if ss_ref is not None:
        r = lax.rsqrt(ss_ref[:, :1] * inv_k + NORM_EPS)
        accs = [a * r for a in accs]
    res = epi(accs, [e[...] for e in e_refs])
    o_refs[0][...] = res.astype(o_refs[0].dtype)
    if gain_ref is not None:
        xg_ref, sso_ref = o_refs[1], o_refs[2]
        xg_ref[...] = (res * gain_ref[...]).astype(xg_ref.dtype)
        part = jnp.broadcast_to(jnp.sum(res * res, axis=-1, keepdims=True), sso_ref.shape)
        j = pl.program_id(1)

        @pl.when(j == 0)
        def _():
            sso_ref[...] = part

        @pl.when(j > 0)
        def _():
            sso_ref[...] += part


def _mm_body(*refs, npf, nx, ws, ne, epi, prenorm, rowscale, norm_out, sidecast, inv_k):
    pf = refs[:npf]
    refs = refs[npf:]
    x_refs, refs = refs[:nx], refs[nx:]
    w_refs, refs = refs[:len(ws)], refs[len(ws):]
    e_refs, refs = refs[:ne], refs[ne:]
    ss_ref = gain_ref = None
    if rowscale:
        ss_ref, refs = refs[0], refs[1:]
    if norm_out:
        gain_ref, refs = refs[0], refs[1:]
    if sidecast:
        side_in, refs = refs[0], refs[1:]
    n_out = 3 if norm_out else 1
    o_refs, refs = refs[:n_out], refs[n_out:]
    if sidecast:
        side_out, refs = refs[0], refs[1:]
        side_out[...] = side_in[...].astype(side_out.dtype)
    if prenorm:
        xn_ref = refs[0]

        @pl.when(pl.program_id(1) == 0)
        def _():
            _rmsnorm_body(x_refs[0], x_refs[1], xn_ref)

        x_refs = (xn_ref,)
    compute = functools.partial(_mm_compute, x_refs, w_refs, e_refs, ss_ref, gain_ref, o_refs, ws, epi, inv_k)
    if npf:
        rows = pf[2][pl.program_id(0)]
        half = o_refs[0].shape[0] // 2
        pl.when(rows > half)(compute)
        pl.when(jnp.logical_and(rows > 0, rows <= half))(functools.partial(compute, nrows=half))

        @pl.when(rows == 0)
        def _():
            o_refs[0][...] = jnp.zeros_like(o_refs[0])
    else:
        compute()


def _active_ij(i, j, n_active, gn):
    return jnp.minimum(i, n_active - 1), jnp.where(i < n_active, j, gn - 1)


def _mm(xs, ws, epi, extras, n, out_dtype, *, tm, tn, group=None, prenorm=None, rowscale=None, norm_out=None,
        sidecast=None, name):
    m = xs[0].shape[0]
    gm, gn = m // tm, n // tn
    npf = 0 if group is None else 3

    def imap(f):
        if group is None:
            return f
        return lambda i, j, g, na, rows: f(*_active_ij(i, j, na[0], gn), g)

    in_specs = []
    blk = 0
    temp = (len(ws) + 2) * tm * tn * 4
    scratch = []
    if prenorm is None:
        for x in xs:
            in_specs.append(pl.BlockSpec((tm, x.shape[1]), imap(lambda i, j, *pf: (i, 0))))
            blk += _nbytes((tm, x.shape[1]), x.dtype)
    else:
        gain, width, col_block = prenorm
        assert len(xs) == 1 and group is None
        in_specs.append(pl.BlockSpec((tm, width), lambda i, j: (i, col_block)))
        in_specs.append(pl.BlockSpec((1, width), lambda i, j: (0, 0)))
        xs = [xs[0], gain.reshape(1, width).astype(F32)]
        blk += _nbytes((tm, width), F32)
        scratch.append(pltpu.VMEM((tm, width), BF16))
        temp += _nbytes((tm, width), BF16) + 2 * _nbytes((tm, width), F32)
    for w in ws:
        lead = () if w.lead is None else (w.lead,)
        none = (None,) * len(lead)
        if w.nt:
            assert group is None and w.arr.ndim == 3
            k = w.arr.shape[-1]
            in_specs.append(pl.BlockSpec((pl.Squeezed(), pl.Element(tn), pl.Element(k)), imap(
                lambda i, j, *pf, lead=w.lead, row0=w.row0: (lead, pl.multiple_of(row0 + j * tn, 8), 0))))
        elif group is None:
            assert w.arr.ndim == 2 + len(lead)
            k = w.arr.shape[-2]
            in_specs.append(pl.BlockSpec(none + (k, tn), imap(
                lambda i, j, *pf, lead=lead, off=w.off: lead + (0, j + off))))
        else:
            assert w.arr.ndim == 3 + len(lead)
            k = w.arr.shape[-2]
            in_specs.append(pl.BlockSpec(none + (None, k, tn), imap(
                lambda i, j, g, lead=lead: lead + (g[i], 0, j))))
        blk += _nbytes((k, tn), w.arr.dtype)
        if w.arr.dtype != BF16:
            temp += _nbytes((k, tn), BF16)
    for arr, kind, off in extras:
        if kind == "mn":
            in_specs.append(pl.BlockSpec((tm, tn), imap(lambda i, j, *pf, off=off: (i, j + off))))
            blk += _nbytes((tm, tn), arr.dtype)
        else:
            in_specs.append(pl.BlockSpec((tm, arr.shape[1]), imap(lambda i, j, *pf: (i, 0))))
            blk += _nbytes((tm, arr.shape[1]), arr.dtype)
    tail = []
    if rowscale is not None:
        assert group is None
        in_specs.append(pl.BlockSpec((tm, V7X_LANES), lambda i, j: (i, 0)))
        blk += _nbytes((tm, V7X_LANES), F32)
        tail.append(rowscale)
    out_specs = [pl.BlockSpec((tm, tn), lambda i, j, *pf: (i, j))]
    out_shape = [jax.ShapeDtypeStruct((m, n), out_dtype)]
    blk += _nbytes((tm, tn), out_dtype)
    if norm_out is not None:
        assert group is None
        in_specs.append(pl.BlockSpec((1, tn), lambda i, j: (0, j)))
        tail.append(norm_out.reshape(1, n).astype(F32))
        out_specs += [pl.BlockSpec((tm, tn), lambda i, j: (i, j)), pl.BlockSpec((tm, V7X_LANES), lambda i, j: (i, 0))]
        out_shape += [jax.ShapeDtypeStruct((m, n), BF16), jax.ShapeDtypeStruct((m, V7X_LANES), F32)]
        blk += _nbytes((tm, tn), BF16) + _nbytes((tm, V7X_LANES), F32)
    if sidecast is not None:
        assert group is None and sidecast.shape[0] % (gm * gn) == 0
        side_blk = (sidecast.shape[0] // (gm * gn), sidecast.shape[1])
        in_specs.append(pl.BlockSpec(side_blk, lambda i, j: (i * gn + j, 0)))
        tail.append(sidecast)
        out_specs.append(pl.BlockSpec(side_blk, lambda i, j: (i * gn + j, 0)))
        out_shape.append(jax.ShapeDtypeStruct(sidecast.shape, BF16))
        blk += _nbytes(side_blk, sidecast.dtype) + _nbytes(side_blk, BF16)
    inv_k = 1.0 / xs[0].shape[1]
    body = functools.partial(_mm_body, npf=npf, nx=len(xs), ws=tuple(w._replace(arr=None) for w in ws),
                             ne=len(extras), epi=epi, prenorm=prenorm is not None,
                             rowscale=rowscale is not None, norm_out=norm_out is not None,
                             sidecast=sidecast is not None, inv_k=inv_k)
    call = pl.pallas_call(
        body,
        grid_spec=pltpu.PrefetchScalarGridSpec(
            num_scalar_prefetch=npf, grid=(gm, gn), in_specs=in_specs, out_specs=out_specs,
            scratch_shapes=scratch),
        out_shape=out_shape,
        compiler_params=pltpu.CompilerParams(
            dimension_semantics=("arbitrary", "arbitrary"), vmem_limit_bytes=_vmem_limit(blk, temp)),
        name=name,
    )
    args = [] if group is None else list(group)
    args += list(xs) + [w.arr for w in ws] + [a for a, _, _ in extras] + tail
    res = call(*args)
    return res if len(res) > 1 else res[0]


def _epi_id(accs, ex):
    return accs[0]


def _epi_gelu(accs, ex):
    a = accs[0]
    return 0.5 * a * (1.0 + lax.erf(a * (2.0 ** -0.5)))


def _sigmoid(a):
    return 0.5 * (1.0 + jnp.tanh(0.5 * a))


def _epi_sigmoid(accs, ex):
    return _sigmoid(accs[0])


def _epi_residual(accs, ex):
    return ex[0] + accs[0]


def _epi_swiglu(accs, ex):
    return accs[0] * _sigmoid(accs[0]) * accs[1]


def _epi_gated_merge(accs, ex):
    return ex[0].astype(F32) * accs[0] + ex[1].astype(F32) * accs[1]


def _rope_lanes(t, cos_t, sin_a, sin_b):
    half = QK_ROPE // 2
    return (t * cos_t + pltpu.roll(t, half, 1) * sin_a
            + pltpu.roll(t, V7X_LANES - half, 1) * sin_b)


def _epi_rope_all(accs, ex):
    return _rope_lanes(accs[0], *ex)


def _epi_q_heads(accs, ex):
    a = accs[0]
    outs = []
    for g in range(a.shape[1] // V7X_LANES):
        t = a[:, g * V7X_LANES:(g + 1) * V7X_LANES]
        outs.append((_rope_lanes(t, *ex) if g % 2 else t) * Q_PRESCALE)
    return jnp.concatenate(outs, axis=1)


def _rmsnorm_body(x_ref, g_ref, o_ref):
    x = x_ref[...].astype(F32)
    y = x * lax.rsqrt(jnp.mean(x * x, axis=-1, keepdims=True) + NORM_EPS)
    o_ref[...] = (y * g_ref[...]).astype(o_ref.dtype)


def _rmsnorm(x, g, out_dtype, *, width=None, col_block=0, name):
    m = x.shape[0]
    width = x.shape[1] if width is None else width
    tm = _tile(m, 256, 8)
    blk = _nbytes((tm, width), x.dtype) + _nbytes((tm, width), out_dtype)
    return pl.pallas_call(
        _rmsnorm_body,
        grid=(m // tm,),
        in_specs=[pl.BlockSpec((tm, width), lambda i: (i, col_block)),
                  pl.BlockSpec((1, width), lambda i: (0, 0))],
        out_specs=pl.BlockSpec((tm, width), lambda i: (i, 0)),
        out_shape=jax.ShapeDtypeStruct((m, width), out_dtype),
        compiler_params=pltpu.CompilerParams(
            dimension_semantics=("arbitrary",), vmem_limit_bytes=_vmem_limit(blk, 3 * tm * width * 4)),
        name=name,
    )(x, g.reshape(1, width).astype(F32))


def _pack_bf16_pairs(h):
    half = h.shape[1] // 2
    bits = lax.bitcast_convert_type(h.astype(BF16).astype(F32), jnp.uint32)
    return bits[:, :half] | (bits[:, half:] >> 16)


def _unpack_bf16_pairs(p):
    hi = lax.bitcast_convert_type(p & jnp.uint32(0xFFFF0000), F32).astype(BF16)
    lo = lax.bitcast_convert_type(p << 16, F32).astype(BF16)
    return hi, lo


def _norm_router_body(x_ref, g_ref, wr_ref, h_ref, route_ref, *, n_experts):
    x = x_ref[...]
    h = x * lax.rsqrt(jnp.mean(x * x, axis=-1, keepdims=True) + NORM_EPS) * g_ref[...]
    h_ref[...] = _pack_bf16_pairs(h)
    wr = wr_ref[...]
    h_hi, w_hi = h.astype(BF16), wr.astype(BF16)
    h_lo = (h - h_hi.astype(F32)).astype(BF16)
    w_lo = (wr - w_hi.astype(F32)).astype(BF16)
    logits = (jnp.dot(h_hi, w_hi, preferred_element_type=F32)
              + (jnp.dot(h_lo, w_hi, preferred_element_type=F32) + jnp.dot(h_hi, w_lo, preferred_element_type=F32)))
    lane = lax.broadcasted_iota(jnp.int32, logits.shape, 1).astype(F32)
    neg = jnp.float32(-jnp.inf)
    far = jnp.float32(V7X_LANES)
    l1 = jnp.where(lane < n_experts, logits, neg)
    m1 = jnp.max(l1, axis=-1, keepdims=True)
    i1 = jnp.min(jnp.where(l1 == m1, lane, far), axis=-1, keepdims=True)
    l2 = jnp.where(lane == i1, neg, l1)
    m2 = jnp.max(l2, axis=-1, keepdims=True)
    i2 = jnp.min(jnp.where(l2 == m2, lane, far), axis=-1, keepdims=True)
    e = jnp.exp(m2 - m1)
    g1 = 1.0 / (1.0 + e)
    g2 = e / (1.0 + e)
    route = jnp.where(lane == 0, i1, jnp.where(lane == 1, i2, jnp.where(lane == 2, g1, jnp.where(lane == 3, g2, 0.0))))
    route_ref[...] = route


def _norm_router(x, g, w_router, *, name):
    m, d = x.shape
    n_experts = w_router.shape[1]
    wr = jnp.pad(w_router.astype(F32), ((0, 0), (0, V7X_LANES - n_experts)))
    tm = _tile(m, 256, 8)
    blk = 2 * _nbytes((tm, d), F32) + _nbytes((d, V7X_LANES), F32) + _nbytes((tm, V7X_LANES), F32)
    return pl.pallas_call(
        functools.partial(_norm_router_body, n_experts=n_experts),
        grid=(m // tm,),
        in_specs=[pl.BlockSpec((tm, d), lambda i: (i, 0)),
                  pl.BlockSpec((1, d), lambda i: (0, 0)),
                  pl.BlockSpec((d, V7X_LANES), lambda i: (0, 0))],
        out_specs=[pl.BlockSpec((tm, d // 2), lambda i: (i, 0)),
                   pl.BlockSpec((tm, V7X_LANES), lambda i: (i, 0))],
        out_shape=[jax.ShapeDtypeStruct((m, d // 2), jnp.uint32), jax.ShapeDtypeStruct((m, V7X_LANES), F32)],
        compiler_params=pltpu.CompilerParams(
            dimension_semantics=("arbitrary",), vmem_limit_bytes=_vmem_limit(blk, 4 * tm * d * 4)),
        name=name,
    )(x, g.reshape(1, d).astype(F32), wr)


def _attn_body(q_ref, kn_ref, kr_ref, v_ref, o_ref, kcat_ref, vone_ref, *, sub):
    @pl.when(pl.program_id(2) == 0)
    def _():
        kcat_ref[:, :QK_NOPE] = kn_ref[...]
        kcat_ref[:, QK_NOPE:] = kr_ref[...]
        vone_ref[:, :V_HEAD] = v_ref[...]
        vone_ref[:, V_HEAD:] = jnp.ones((v_ref.shape[0], V_HEAD), vone_ref.dtype)

    for r0 in range(0, q_ref.shape[0], sub):
        rows = slice(r0, r0 + sub)
        s = lax.dot_general(q_ref[rows, :], kcat_ref[...], (((1,), (1,)), ((), ())),
                            preferred_element_type=F32)
        p = jnp.exp2(s - jnp.max(s, axis=-1, keepdims=True))
        ol = jnp.dot(p.astype(BF16), vone_ref[...], preferred_element_type=F32)
        o_ref[rows, :] = (ol[:, :V_HEAD] / ol[:, V_HEAD:]).astype(o_ref.dtype)


def _attention(q, kv, k_rope, *, batch, seq, heads, name):
    t = batch * seq
    sub = _tile(seq, 256, 8)
    tq = _tile(seq, 8 * sub, sub)
    nq = seq // tq
    blk = (_nbytes((tq, HEAD_PAD), BF16) + 3 * _nbytes((seq, V7X_LANES), BF16) + _nbytes((tq, V_HEAD), BF16))
    return pl.pallas_call(
        functools.partial(_attn_body, sub=sub),
        grid=(batch, heads, nq),
        in_specs=[pl.BlockSpec((tq, HEAD_PAD), lambda b, h, i: (b * nq + i, h)),
                  pl.BlockSpec((seq, QK_NOPE), lambda b, h, i: (b, 2 * h)),
                  pl.BlockSpec((seq, V7X_LANES), lambda b, h, i: (b, 0)),
                  pl.BlockSpec((seq, V_HEAD), lambda b, h, i: (b, 2 * h + 1))],
        out_specs=pl.BlockSpec((tq, V_HEAD), lambda b, h, i: (b * nq + i, h)),
        out_shape=jax.ShapeDtypeStruct((t, heads * V_HEAD), BF16),
        scratch_shapes=[pltpu.VMEM((seq, HEAD_PAD), BF16), pltpu.VMEM((seq, 2 * V_HEAD), BF16)],
        compiler_params=pltpu.CompilerParams(
            dimension_semantics=("arbitrary", "arbitrary", "arbitrary"),
            vmem_limit_bytes=_vmem_limit(blk, _nbytes((seq, HEAD_PAD), BF16) + 4 * tq * seq * 4)),
        name=name,
    )(q, kv, k_rope, kv)


def _gmlp_body(u_ref, v_ref, g_ref, b_ref, w_ref, bs_ref, o_ref, *, n_chunks, n_groups):
    v = v_ref[...].astype(F32)
    mu = jnp.mean(v, axis=-1, keepdims=True)
    vc = v - mu
    vn = vc * lax.rsqrt(jnp.mean(vc * vc, axis=-1, keepdims=True) + NORM_EPS)
    vn = (vn * g_ref[...] + b_ref[...]).astype(BF16)
    for c in range(n_chunks):
        rows = slice(c * CHUNK, (c + 1) * CHUNK)
        for g in range(n_groups):
            cols = slice(g * GROUP_DIM, (g + 1) * GROUP_DIM)
            s = jnp.dot(w_ref[g], vn[rows, cols], preferred_element_type=F32) + bs_ref[g]
            o_ref[rows, cols] = (u_ref[rows, cols].astype(F32) * s).astype(o_ref.dtype)


def _gmlp(uv, v_norm_g, v_norm_b, w_sp, b_sp, *, name):
    t = uv.shape[0]
    gw = uv.shape[1] // 2
    n_groups = w_sp.shape[0]
    rows = _tile(t, 4 * CHUNK, CHUNK)
    b_full = jnp.broadcast_to(b_sp.astype(F32)[:, :, None], (n_groups, CHUNK, GROUP_DIM))
    blk = 3 * _nbytes((rows, gw), BF16) + _nbytes(w_sp.shape, BF16) + _nbytes(b_full.shape, F32)
    return pl.pallas_call(
        functools.partial(_gmlp_body, n_chunks=rows // CHUNK, n_groups=n_groups),
        grid=(t // rows,),
        in_specs=[pl.BlockSpec((rows, gw), lambda i: (i, 0)),
                  pl.BlockSpec((rows, gw), lambda i: (i, 1)),
                  pl.BlockSpec((1, gw), lambda i: (0, 0)),
                  pl.BlockSpec((1, gw), lambda i: (0, 0)),
                  pl.BlockSpec((n_groups, CHUNK, CHUNK), lambda i: (0, 0, 0)),
                  pl.BlockSpec((n_groups, CHUNK, GROUP_DIM), lambda i: (0, 0, 0))],
        out_specs=pl.BlockSpec((rows, gw), lambda i: (i, 0)),
        out_shape=jax.ShapeDtypeStruct((t, gw), BF16),
        compiler_params=pltpu.CompilerParams(
            dimension_semantics=("arbitrary",), vmem_limit_bytes=_vmem_limit(blk, 4 * rows * gw * 4)),
        name=name,
    )(uv, uv, v_norm_g.reshape(1, gw).astype(F32), v_norm_b.reshape(1, gw).astype(F32),
      w_sp.astype(BF16), b_full)


def _row_copy(src_hbm, dst_vmem, src_row, dst_row, sem):
    return pltpu.make_async_copy(src_hbm.at[pl.ds(src_row, 1)], dst_vmem.at[pl.ds(dst_row, 1)], sem)


def _moe_up_body(exp_ref, nact_ref, rows_ref, tok_ref, hp_hbm, w1_ref, w3_ref, o_ref, xbuf, xbf, sem, *, tile, gn):
    r = pl.program_id(0)
    j = pl.program_id(1)
    n_active = nact_ref[0]
    chunk = tile // gn
    half = xbuf.shape[2]

    def issue(row_tile, slot, row0, n_rows):
        base = row_tile * tile

        def f(k, c):
            row = row0 + k
            _row_copy(hp_hbm, xbuf.at[slot], tok_ref[base + row], row, sem.at[slot]).start()
            return c

        lax.fori_loop(0, n_rows, f, 0, unroll=8)

    @pl.when(jnp.logical_and(r == 0, j == 0))
    def _():
        issue(0, 0, 0, tile)

    @pl.when(jnp.logical_and(r < n_active, j == 0))
    def _():
        slot = r % 2

        def drain(k, c):
            _row_copy(hp_hbm, xbuf.at[slot], 0, k, sem.at[slot]).wait()
            return c

        lax.fori_loop(0, tile, drain, 0, unroll=8)
        hi, lo = _unpack_bf16_pairs(xbuf[slot])
        xbf[:, :half] = hi
        xbf[:, half:] = lo

    def compute(nrows):
        x = xbf[:nrows, :]
        a = jnp.dot(x, w1_ref[...].astype(BF16), preferred_element_type=F32)
        b = jnp.dot(x, w3_ref[...].astype(BF16), preferred_element_type=F32)
        o_ref[:nrows, :] = (a * _sigmoid(a) * b).astype(o_ref.dtype)
        if nrows < tile:
            o_ref[nrows:, :] = jnp.zeros((tile - nrows, o_ref.shape[1]), o_ref.dtype)

    rows = rows_ref[r]
    pl.when(rows > tile // 2)(functools.partial(compute, tile))
    pl.when(jnp.logical_and(rows > 0, rows <= tile // 2))(functools.partial(compute, tile // 2))

    @pl.when(rows == 0)
    def _():
        o_ref[...] = jnp.zeros_like(o_ref)

    @pl.when(r + 1 < n_active)
    def _():
        issue(r + 1, (r + 1) % 2, j * chunk, chunk)


def _moe_up(hp, slot_tok, group, w1, w3, lead, *, tile, tn, name):
    tile_exp, n_active, tile_rows = group
    n_slots = slot_tok.shape[0]
    half = hp.shape[1]
    d, f = w1.shape[-2], w1.shape[-1]
    gn = f // tn
    assert tile % gn == 0

    def w_map(r, j, e, na, rows, tok):
        re, je = _active_ij(r, j, na[0], gn)
        return (lead, e[re], 0, je)

    blk = 2 * _nbytes((d, tn), w1.dtype) + _nbytes((tile, tn), BF16)
    temp = (2 * _nbytes((tile, half), jnp.uint32) + _nbytes((tile, d), BF16)
            + 2 * _nbytes((d, tn), BF16) + 4 * tile * tn * 4)
    return pl.pallas_call(
        functools.partial(_moe_up_body, tile=tile, gn=gn),
        grid_spec=pltpu.PrefetchScalarGridSpec(
            num_scalar_prefetch=4, grid=(n_slots // tile, gn),
            in_specs=[pl.BlockSpec(memory_space=pl.ANY),
                      pl.BlockSpec((None, None, d, tn), w_map),
                      pl.BlockSpec((None, None, d, tn), w_map)],
            out_specs=pl.BlockSpec((tile, tn), lambda r, j, *pf: (r, j)),
            scratch_shapes=[pltpu.VMEM((2, tile, half), jnp.uint32), pltpu.VMEM((tile, d), BF16),
                            pltpu.SemaphoreType.DMA((2,))]),
        out_shape=jax.ShapeDtypeStruct((n_slots, f), BF16),
        compiler_params=pltpu.CompilerParams(
            dimension_semantics=("arbitrary", "arbitrary"), vmem_limit_bytes=_vmem_limit(blk, temp)),
        name=name,
    )(tile_exp, n_active, tile_rows, slot_tok, hp, w1, w3)


def _combine_body(slot_ref, x_ref, route_ref, g_ref, y_hbm, o_ref, buf, sem, *, rows, n_tiles, norm):
    i = pl.program_id(0)

    def issue(tile, slot):
        base = tile * rows

        def f(k, c):
            for j in range(TOP_K):
                _row_copy(y_hbm, buf.at[slot, j], slot_ref[(base + k) * TOP_K + j], k, sem.at[slot]).start()
            return c

        lax.fori_loop(0, rows, f, 0, unroll=4)

    @pl.when(i == 0)
    def _():
        issue(0, 0)

    @pl.when(i + 1 < n_tiles)
    def _():
        issue(i + 1, (i + 1) % 2)

    slot = i % 2

    def drain(k, c):
        for j in range(TOP_K):
            _row_copy(y_hbm, buf.at[slot, j], 0, k, sem.at[slot]).wait()
        return c

    lax.fori_loop(0, rows, drain, 0, unroll=4)
    route = route_ref[...]
    y = x_ref[...] + (buf[slot, 0] * route[:, 2:3] + buf[slot, 1] * route[:, 3:4])
    if norm:
        y = y * lax.rsqrt(jnp.mean(y * y, axis=-1, keepdims=True) + NORM_EPS) * g_ref[...]
    o_ref[...] = y


def _combine(x, route, slot_of_assign, yb, norm_gain, *, name):
    t, d = x.shape
    rows = _tile(t, 256, 8)
    n_tiles = t // rows
    norm = norm_gain is not None
    g = (norm_gain if norm else jnp.ones((d,), F32)).reshape(1, d).astype(F32)
    blk = 2 * _nbytes((rows, d), F32) + _nbytes((rows, V7X_LANES), F32)
    return pl.pallas_call(
        functools.partial(_combine_body, rows=rows, n_tiles=n_tiles, norm=norm),
        grid_spec=pltpu.PrefetchScalarGridSpec(
            num_scalar_prefetch=1, grid=(n_tiles,),
            in_specs=[pl.BlockSpec((rows, d), lambda i, s: (i, 0)),
                      pl.BlockSpec((rows, V7X_LANES), lambda i, s: (i, 0)),
                      pl.BlockSpec((1, d), lambda i, s: (0, 0)),
                      pl.BlockSpec(memory_space=pl.ANY)],
            out_specs=pl.BlockSpec((rows, d), lambda i, s: (i, 0)),
            scratch_shapes=[pltpu.VMEM((2, TOP_K, rows, d), F32), pltpu.SemaphoreType.DMA((2,))]),
        out_shape=jax.ShapeDtypeStruct((t, d), F32),
        compiler_params=pltpu.CompilerParams(
            dimension_semantics=("arbitrary",),
            vmem_limit_bytes=_vmem_limit(blk, (2 * TOP_K + 3) * _nbytes((rows, d), F32))),
        name=name,
    )(slot_of_assign, x, route, g, yb)


def _routing_tables(route, n_experts, tile):
    t = route.shape[0]
    a = t * TOP_K
    flat_e = route[:, :TOP_K].astype(jnp.int32).reshape(a)
    order = jnp.argsort(flat_e).astype(jnp.int32)
    rank = jnp.argsort(order).astype(jnp.int32)
    counts = jnp.sum(flat_e[:, None] == jnp.arange(n_experts, dtype=jnp.int32)[None, :], axis=0, dtype=jnp.int32)
    padded = (counts + tile - 1) // tile * tile
    pad_end = jnp.cumsum(padded)
    pad_start = pad_end - padded
    shift = pad_start - (jnp.cumsum(counts) - counts)
    slot_of_assign = rank + shift[flat_e]
    n_tiles = -(-a // tile) + n_experts
    tile_start = jnp.arange(n_tiles, dtype=jnp.int32) * tile
    tile_exp = jnp.minimum(jnp.searchsorted(pad_end, tile_start, side="right"),
                           n_experts - 1).astype(jnp.int32)
    slot = jnp.arange(n_tiles * tile, dtype=jnp.int32)
    slot_exp = jnp.repeat(tile_exp, tile)
    real = slot - pad_start[slot_exp] < counts[slot_exp]
    slot_tok = jnp.where(real, order[jnp.clip(slot - shift[slot_exp], 0, a - 1)] // TOP_K, 0)
    n_active = (pad_end[-1:] // tile).astype(jnp.int32)
    tile_rows = jnp.clip(counts[tile_exp] - (tile_start - pad_start[tile_exp]), 0, tile).astype(jnp.int32)
    return slot_tok.astype(jnp.int32), slot_of_assign.astype(jnp.int32), (tile_exp, n_active, tile_rows)


def _rope_panels(positions):
    inv_freq = ROPE_THETA ** (-jnp.arange(0, QK_ROPE, 2, dtype=F32) / QK_ROPE)
    ang = positions.astype(F32).reshape(-1, 1) * inv_freq
    cos, sin = jnp.cos(ang), jnp.sin(ang)
    z = jnp.zeros_like(cos)
    pad = jnp.zeros((cos.shape[0], V7X_LANES - QK_ROPE), F32)
    cos_t = jnp.concatenate([cos, cos, pad], axis=1)
    sin_a = jnp.concatenate([z, sin, pad], axis=1)
    sin_b = jnp.concatenate([-sin, z, pad], axis=1)
    return cos_t, sin_a, sin_b


def kernel(x, positions, mix_norm, w_in, q_norm, kv_norm, w_uq, w_ukv, v_norm_g, v_norm_b, w_sp, b_sp, w_branch_a, w_branch_b, w_out, ffn_norm, dense_w1, dense_w3, dense_w2, w_router, moe_w1, moe_w3, moe_w2, final_norm):
    batch, seq, d = x.shape
    t = batch * seq
    depth = mix_norm.shape[0]
    ql, kvl = q_norm.shape[1], kv_norm.shape[1]
    heads = w_ukv.shape[2] // (QK_NOPE + V_HEAD)
    gw = v_norm_g.shape[1]
    o2 = ql + kvl
    o3 = o2 + QK_ROPE
    o4 = o3 + 2 * gw
    assert ql % kvl == 0
    tm = _tile(t, 1024, 8)
    tn_d = _tile(d, 512)
    w_in_t = jnp.swapaxes(w_in, 1, 2)

    rope = _rope_panels(positions)
    rope_ex = [(p, "m", 0) for p in rope]
    xf = x.reshape(t, d)
    out = h = h_ss = None
    for layer in range(depth):
        w_q = jnp.pad(w_uq[layer].reshape(ql, heads, QK_NOPE + QK_ROPE),
                      ((0, 0), (0, 0), (0, HEAD_PAD - QK_NOPE - QK_ROPE))).reshape(ql, heads * HEAD_PAD).astype(BF16)

        if h is None:
            h, h_ss = _rmsnorm(xf, mix_norm[layer], BF16, name=f"mix_norm{layer}"), None
        cqkv = _mm([h], [W(0, w_in_t, layer, nt=True)], _epi_id, [], o2, F32,
                   tm=tm, tn=_tile(o2, 512), rowscale=h_ss, name=f"in_latent{layer}")
        k_rope = _mm([h], [W(0, w_in_t, layer, nt=True, row0=o2)], _epi_rope_all, rope_ex, V7X_LANES, BF16,
                     tm=_tile(t, 2048, 8), tn=V7X_LANES, rowscale=h_ss, name=f"in_krope{layer}")
        uv = _mm([h], [W(0, w_in_t, layer, nt=True, row0=o3)], _epi_gelu, [], 2 * gw, BF16,
                 tm=tm, tn=_tile(2 * gw, 512), rowscale=h_ss, name=f"in_uv{layer}")
        gates = _mm([h], [W(0, w_in_t, layer, nt=True, row0=o4)], _epi_sigmoid, [], 2 * d, BF16,
                    tm=tm, tn=_tile(2 * d, 512), rowscale=h_ss, name=f"in_gate{layer}")
        h = None

        q = _mm([cqkv], [W(0, w_q)], _epi_q_heads, rope_ex, heads * HEAD_PAD, BF16,
                tm=tm, tn=_tile(heads * HEAD_PAD, 2048, 2 * V7X_LANES),
                prenorm=(q_norm[layer], ql, 0), name=f"q_up{layer}")
        kv = _mm([cqkv], [W(0, w_ukv[layer].astype(BF16))], _epi_id, [], heads * (QK_NOPE + V_HEAD), BF16,
                 tm=tm, tn=_tile(heads * (QK_NOPE + V_HEAD), 2048),
                 prenorm=(kv_norm[layer], kvl, ql // kvl), name=f"kv_up{layer}")
        y_a = _attention(q, kv, k_rope, batch=batch, seq=seq, heads=heads, name=f"attention{layer}")
        y_b = _gmlp(uv, v_norm_g[layer], v_norm_b[layer], w_sp[layer], b_sp[layer], name=f"gmlp{layer}")

        merged = _mm([y_a, y_b], [W(0, w_branch_a, layer), W(1, w_branch_b, layer)],
                     _epi_gated_merge, [(gates, "mn", 0), (gates, "mn", d // tn_d)], d, BF16,
                     tm=tm, tn=tn_d, name=f"merge{layer}")
        i = layer // 2
        last = layer == depth - 1
        dense = layer % 2 == 0
        xf = _mm([merged], [W(0, w_out, layer)], _epi_residual, [(xf, "mn", 0)], d, F32,
                 tm=tm, tn=tn_d, norm_out=ffn_norm[layer] if dense else None, name=f"mix_out{layer}")
        if dense:
            ff = dense_w1.shape[2]
            xf, hg, hg_ss = xf
            act, w2 = _mm([hg], [W(0, dense_w1, i), W(0, dense_w3, i)], _epi_swiglu, [], ff, BF16,
                          tm=tm, tn=_tile(ff, 256), rowscale=hg_ss, sidecast=dense_w2[i], name=f"dense_up{layer}")
            xf = _mm([act], [W(0, w2)], _epi_residual, [(xf, "mn", 0)], d, F32,
                     tm=_tile(t, 512, 8), tn=_tile(d, 512),
                     norm_out=None if last else mix_norm[layer + 1], name=f"dense_down{layer}")
            if last:
                out = _rmsnorm(xf, final_norm, F32, name="final_norm")
            else:
                xf, h, h_ss = xf
        else:
            n_experts = w_router.shape[2]
            fe = moe_w1.shape[3]
            hp, route = _norm_router(xf, ffn_norm[layer], w_router[i], name=f"ffn_norm_router{layer}")
            slot_tok, slot_of_assign, group = _routing_tables(route, n_experts, MOE_TILE)
            act = _moe_up(hp, slot_tok, group, moe_w1, moe_w3, i,
                          tile=MOE_TILE, tn=_tile(fe, 256), name=f"moe_up{layer}")
            yb = _mm([act], [W(0, moe_w2, i)], _epi_id, [], d, F32,
                     tm=MOE_TILE, tn=tn_d, group=group, name=f"moe_down{layer}")
            xf = _combine(xf, route, slot_of_assign, yb, final_norm if last else None, name=f"moe_combine{layer}")
            if last:
                out = xf
    return out.reshape(batch, seq, d)
```
